```python
import jax, jax.numpy as jnp
from jax import lax
import numpy as np

D_MODEL = 1024
BATCH = 4
SEQ = 4096
DEPTH = 4
DEC_BATCH = 128
DEC_SEQ = 8
PAST_LEN = 2048
PAGE_SIZE = 128

N_A_LAYERS = DEPTH // 2
N_B_LAYERS = DEPTH - N_A_LAYERS
MIX_WIDTH = D_MODEL
MEM_LEN = 256
H_MEM = 4
HD_MEM = 64
D_MEM = H_MEM * HD_MEM
D_MAIN = MIX_WIDTH - D_MEM
H_A = 4
HD_A = D_MAIN // H_A
H_B = 12
HD_B = D_MAIN // H_B
CONV_W = 4
D_FF = ((8 * D_MODEL // 3 + 127) // 128) * 128
CHUNK = 64
Q_BLOCK = 128
EPS = 1e-6
F_BIAS = 3.0
SB_BIAS_INIT = -6.0
D_IN_A = 4 * D_MAIN + 2 * H_A + D_MEM
D_IN_B = D_MAIN + D_MEM

kernel_name = "yoco_mlstm_stickbreak_macaron_step"


def rms_norm(x, g):
    xf = x.astype(jnp.float32)
    y = xf * lax.rsqrt(jnp.mean(xf * xf, axis=-1, keepdims=True) + EPS)
    return (y * g.astype(jnp.float32)).astype(x.dtype)


def swiglu(h, w_up, w_down):
    g, u = jnp.split(h @ w_up, 2, axis=-1)
    return (jax.nn.silu(g) * u) @ w_down


def memory_kv(mem, g_mem, w_mem_kv, g_k_mem):
    B, M, _ = mem.shape
    mk, mv = jnp.split(rms_norm(mem, g_mem) @ w_mem_kv, 2, axis=-1)
    mk = rms_norm(mk.reshape(B, M, H_MEM, HD_MEM), g_k_mem)
    return mk, mv.reshape(B, M, H_MEM, HD_MEM)


def memory_attention(qm, g_q_mem, mem_k, mem_v):
    B, L, _ = qm.shape
    q = rms_norm(qm.reshape(B, L, H_MEM, HD_MEM), g_q_mem)
    s = jnp.einsum('blhd,bmhd->bhlm', q, mem_k.astype(q.dtype)).astype(jnp.float32) * (HD_MEM ** -0.5)
    p = jax.nn.softmax(s, axis=-1).astype(qm.dtype)
    return jnp.einsum('bhlm,bmhd->blhd', p, mem_v.astype(qm.dtype)).reshape(B, L, D_MEM)


def mlstm_chunk(carry, xs):
    C, n, m = carry
    q, k, v, ig, lf = xs
    L = q.shape[1]
    b = jnp.cumsum(lf, axis=1).transpose(0, 2, 1)
    i_ = ig.transpose(0, 2, 1)
    causal = jnp.tril(jnp.ones((L, L), dtype=bool))
    dlog = jnp.where(causal, b[..., :, None] - b[..., None, :] + i_[..., None, :], -jnp.inf)
    inter = b + m[..., None]
    m_t = jnp.maximum(inter, jnp.max(dlog, axis=-1))
    w = jnp.exp(dlog - m_t[..., None]) * jnp.einsum('bthd,bshd->bhts', q, k)
    w_inter = jnp.exp(inter - m_t)
    cq = jnp.einsum('bhed,bthd->bthe', C, q) * w_inter.transpose(0, 2, 1)[..., None]
    num = jnp.einsum('bhts,bshd->bthd', w, v) + cq
    den = jnp.sum(w, axis=-1) + w_inter * jnp.einsum('bhd,bthd->bht', n, q)
    den = jnp.maximum(jnp.abs(den), jnp.exp(-m_t)).transpose(0, 2, 1)[..., None]
    h = num / den
    m_new = m_t[..., -1]
    w_end = jnp.exp(b[..., -1:] - b + i_ - m_new[..., None])
    decay = jnp.exp(b[..., -1] + m - m_new)
    C_new = decay[..., None, None] * C + jnp.einsum('bhs,bshe,bshd->bhed', w_end, v, k)
    n_new = decay[..., None] * n + jnp.einsum('bhs,bshd->bhd', w_end, k)
    return (C_new, n_new, m_new), h


def mlstm_scan(q, k, v, ig, lf, C0, n0, m0):
    B, L, H, D = q.shape
    chunk = CHUNK if L % CHUNK == 0 else L
    nc = L // chunk

    def to_chunks(a):
        return a.reshape((B, nc, chunk) + a.shape[2:]).swapaxes(0, 1)

    init = (C0.astype(jnp.float32), n0.astype(jnp.float32), m0.astype(jnp.float32))
    carry, h = lax.scan(mlstm_chunk, init, tuple(to_chunks(a) for a in (q, k, v, ig, lf)))
    return h.swapaxes(0, 1).reshape(B, L, H, D), carry


def mlstm_mix(h, mem_k, mem_v, w_in, b_gate, conv_w, conv_b, g_h, g_q_mem, conv_prev, C0, n0, m0):
    B, L, _ = h.shape
    qk_pre, v, o_pre, gates, qm = jnp.split(h @ w_in, [2 * D_MAIN, 3 * D_MAIN, 4 * D_MAIN, 4 * D_MAIN + 2 * H_A], axis=-1)
    gates = gates.astype(jnp.float32) + b_gate.astype(jnp.float32)
    ig = gates[..., :H_A]
    lf = jax.nn.log_sigmoid(gates[..., H_A:])
    buf = jnp.concatenate([conv_prev.astype(qk_pre.dtype), qk_pre], axis=1)
    qk = conv_b
    for j in range(CONV_W):
        qk = qk + buf[:, j:j + L] * conv_w[j]
    qk = jax.nn.silu(qk)
    new_conv = buf[:, buf.shape[1] - (CONV_W - 1):]
    q, k = jnp.split(qk.astype(jnp.float32), 2, axis=-1)
    q = q.reshape(B, L, H_A, HD_A)
    k = k.reshape(B, L, H_A, HD_A) * (HD_A ** -0.5)
    v = v.astype(jnp.float32).reshape(B, L, H_A, HD_A)
    hA, (C, n, m) = mlstm_scan(q, k, v, ig, lf, C0, n0, m0)
    o = jax.nn.sigmoid(o_pre.astype(jnp.float32)).reshape(B, L, H_A, HD_A)
    hA = (rms_norm(hA, g_h) * o).reshape(B, L, D_MAIN).astype(h.dtype)
    mem_out = memory_attention(qm, g_q_mem, mem_k, mem_v)
    return jnp.concatenate([hA, mem_out], axis=-1), new_conv, C, n, m


def stick_breaking(q, k, v, q_offset, bias):
    B, Lq, H, D = q.shape
    Lk = k.shape[1]
    blk = Q_BLOCK if Lq % Q_BLOCK == 0 else Lq
    nb = Lq // blk
    k_pos = jnp.arange(Lk, dtype=jnp.int32)
    scale = HD_B ** -0.5
    bias = bias.astype(jnp.float32)[None, :, None, None]

    def block(args):
        qb, start = args
        q_pos = start + jnp.arange(blk, dtype=jnp.int32)
        mask = k_pos[None, :] < q_pos[:, None]
        z = jnp.einsum('bqhd,bkhd->bhqk', qb, k).astype(jnp.float32) * scale + bias
        log_keep = jnp.where(mask, jax.nn.log_sigmoid(-z), 0.0)
        later = lax.cumsum(log_keep, axis=3, reverse=True) - log_keep
        a = jnp.where(mask, jnp.exp(jax.nn.log_sigmoid(z) + later), 0.0)
        return jnp.einsum('bhqk,bkhd->bqhd', a.astype(v.dtype), v)

    qs = q.reshape(B, nb, blk, H, D).transpose(1, 0, 2, 3, 4)
    starts = q_offset + jnp.arange(nb, dtype=jnp.int32) * blk
    out = lax.map(block, (qs, starts))
    return out.transpose(1, 0, 2, 3, 4).reshape(B, Lq, H * D)


def sb_mix(h, k_all, v_all, q_offset, mem_k, mem_v, w_in, g_q_b, b_sb, g_q_mem):
    B, L, _ = h.shape
    qb, qm = jnp.split(h @ w_in, [D_MAIN], axis=-1)
    qb = rms_norm(qb.reshape(B, L, H_B, HD_B), g_q_b)
    out_b = stick_breaking(qb, k_all, v_all, q_offset, b_sb)
    mem_out = memory_attention(qm, g_q_mem, mem_k, mem_v)
    return jnp.concatenate([out_b.astype(h.dtype), mem_out], axis=-1)


def shared_kv(x, g_kv, w_kv, g_k_b):
    B, L, _ = x.shape
    k, v = jnp.split(rms_norm(x, g_kv) @ w_kv, 2, axis=-1)
    k = rms_norm(k.reshape(B, L, H_B, HD_B), g_k_b)
    return k, v.reshape(B, L, H_B, HD_B)


def run_trunk(x, mem_k, mem_v, conv0, C0, n0, m0, past_k, past_v,
              g_ffn1, w_ffn1_up, w_ffn1_down, g_mix, w_in_a, b_gate_a, conv_w_a, conv_b_a, g_h_a,
              g_kv, w_kv, g_k_b, w_in_b, g_q_b, b_sb, g_q_mem, w_out, g_ffn2, w_ffn2_up, w_ffn2_down):
    q_offset = past_k.shape[1]
    k_new = v_new = k_all = v_all = None
    convs, Cs, ns, ms = [], [], [], []
    for l in range(DEPTH):
        if l == N_A_LAYERS:
            k_new, v_new = shared_kv(x, g_kv, w_kv, g_k_b)
            k_all = jnp.concatenate([past_k.astype(k_new.dtype), k_new], axis=1)
            v_all = jnp.concatenate([past_v.astype(v_new.dtype), v_new], axis=1)
        x = x + 0.5 * swiglu(rms_norm(x, g_ffn1[l]), w_ffn1_up[l], w_ffn1_down[l])
        h = rms_norm(x, g_mix[l])
        if l < N_A_LAYERS:
            mix, cv, C, n, m = mlstm_mix(h, mem_k[l], mem_v[l], w_in_a[l], b_gate_a[l], conv_w_a[l], conv_b_a[l],
                                         g_h_a[l], g_q_mem[l], conv0[l], C0[l], n0[l], m0[l])
            convs.append(cv)
            Cs.append(C)
            ns.append(n)
            ms.append(m)
        else:
            j = l - N_A_LAYERS
            mix = sb_mix(h, k_all, v_all, q_offset, mem_k[l], mem_v[l], w_in_b[j], g_q_b[j], b_sb[j], g_q_mem[l])
        x = x + mix @ w_out[l]
        x = x + 0.5 * swiglu(rms_norm(x, g_ffn2[l]), w_ffn2_up[l], w_ffn2_down[l])
    return x, k_new, v_new, jnp.stack(convs), jnp.stack(Cs), jnp.stack(ns), jnp.stack(ms)


def setup_inputs(seed: int = 0) -> dict:
    key = jax.random.key(seed)
    ks = list(jax.random.split(key, 64))

    def nrm(shape, scale=1.0):
        return jax.random.normal(ks.pop(), shape, jnp.float32) * scale

    def gain(shape):
        return 1.0 + 0.05 * jax.random.normal(ks.pop(), shape, jnp.float32)

    n_pages = PAST_LEN // PAGE_SIZE
    n_used = DEC_BATCH * n_pages
    n_phys = n_used + max(1, n_used // 4)
    page_table = jax.random.permutation(ks.pop(), n_phys)[:n_used].reshape(DEC_BATCH, n_pages).astype(jnp.int32)
    b_gate_a = jnp.concatenate([nrm((N_A_LAYERS, H_A), 0.1), F_BIAS + nrm((N_A_LAYERS, H_A), 0.5)], axis=-1)
    d = D_MODEL ** -0.5
    return {
        "x_prompt": nrm((BATCH, SEQ, D_MODEL)),
        "x_sample": nrm((DEC_BATCH, DEC_SEQ, D_MODEL)),
        "cache_k": nrm((n_phys, PAGE_SIZE, H_B, HD_B)),
        "cache_v": nrm((n_phys, PAGE_SIZE, H_B, HD_B)),
        "cache_mem_k": nrm((DEPTH, DEC_BATCH, MEM_LEN, H_MEM, HD_MEM)),
        "cache_mem_v": nrm((DEPTH, DEC_BATCH, MEM_LEN, H_MEM, HD_MEM)),
        "state_C": nrm((N_A_LAYERS, DEC_BATCH, H_A, HD_A, HD_A), 0.05),
        "state_n": nrm((N_A_LAYERS, DEC_BATCH, H_A, HD_A), 0.5),
        "state_m": nrm((N_A_LAYERS, DEC_BATCH, H_A), 0.5),
        "state_conv": nrm((N_A_LAYERS, DEC_BATCH, CONV_W - 1, 2 * D_MAIN)),
        "page_table": page_table,
        "mem_prompt": nrm((BATCH, MEM_LEN, D_MODEL)),
        "g_ffn1": gain((DEPTH, D_MODEL)),
        "w_ffn1_up": nrm((DEPTH, D_MODEL, 2 * D_FF), d),
        "w_ffn1_down": nrm((DEPTH, D_FF, D_MODEL), D_FF ** -0.5),
        "g_mix": gain((DEPTH, D_MODEL)),
        "w_in_a": nrm((N_A_LAYERS, D_MODEL, D_IN_A), d),
        "b_gate_a": b_gate_a,
        "conv_w_a": nrm((N_A_LAYERS, CONV_W, 2 * D_MAIN), CONV_W ** -0.5),
        "conv_b_a": nrm((N_A_LAYERS, 2 * D_MAIN), 0.01),
        "g_h_a": gain((N_A_LAYERS, H_A, HD_A)),
        "g_kv": gain((D_MODEL,)),
        "w_kv": nrm((D_MODEL, 2 * D_MAIN), d),
        "g_k_b": gain((HD_B,)),
        "w_in_b": nrm((N_B_LAYERS, D_MODEL, D_IN_B), d),
        "g_q_b": gain((N_B_LAYERS, HD_B)),
        "b_sb": SB_BIAS_INIT + nrm((N_B_LAYERS, H_B), 0.5),
        "g_mem": gain((DEPTH, D_MODEL)),
        "w_mem_kv": nrm((DEPTH, D_MODEL, 2 * D_MEM), d),
        "g_q_mem": gain((DEPTH, HD_MEM)),
        "g_k_mem": gain((DEPTH, HD_MEM)),
        "w_out": nrm((DEPTH, MIX_WIDTH, D_MODEL), MIX_WIDTH ** -0.5),
        "g_ffn2": gain((DEPTH, D_MODEL)),
        "w_ffn2_up": nrm((DEPTH, D_MODEL, 2 * D_FF), d),
        "w_ffn2_down": nrm((DEPTH, D_FF, D_MODEL), D_FF ** -0.5),
    }


def reference(x_prompt, x_sample, cache_k, cache_v, cache_mem_k, cache_mem_v, state_C, state_n, state_m,
              state_conv, page_table, mem_prompt, g_ffn1, w_ffn1_up, w_ffn1_down, g_mix, w_in_a, b_gate_a,
              conv_w_a, conv_b_a, g_h_a, g_kv, w_kv, g_k_b, w_in_b, g_q_b, b_sb, g_mem, w_mem_kv, g_q_mem, g_k_mem,
              w_out, g_ffn2, w_ffn2_up, w_ffn2_down):
    mks, mvs = [], []
    for l in range(DEPTH):
        mk, mv = memory_kv(mem_prompt, g_mem[l], w_mem_kv[l], g_k_mem[l])
        mks.append(mk)
        mvs.append(mv)
    mem_k_prompt = jnp.stack(mks)
    mem_v_prompt = jnp.stack(mvs)
    Bp = x_prompt.shape[0]
    conv0 = jnp.zeros((N_A_LAYERS, Bp, CONV_W - 1, 2 * D_MAIN), x_prompt.dtype)
    C0 = jnp.zeros((N_A_LAYERS, Bp, H_A, HD_A, HD_A), jnp.float32)
    n0 = jnp.zeros((N_A_LAYERS, Bp, H_A, HD_A), jnp.float32)
    m0 = jnp.zeros((N_A_LAYERS, Bp, H_A), jnp.float32)
    empty = jnp.zeros((Bp, 0, H_B, HD_B), x_prompt.dtype)
    y_prompt, k_prompt, v_prompt, conv_prompt, C_prompt, n_prompt, m_prompt = run_trunk(
        x_prompt, mem_k_prompt, mem_v_prompt, conv0, C0, n0, m0, empty, empty,
        g_ffn1, w_ffn1_up, w_ffn1_down, g_mix, w_in_a, b_gate_a, conv_w_a, conv_b_a, g_h_a,
        g_kv, w_kv, g_k_b, w_in_b, g_q_b, b_sb, g_q_mem, w_out, g_ffn2, w_ffn2_up, w_ffn2_down)
    Bs = x_sample.shape[0]
    past_len = page_table.shape[1] * cache_k.shape[1]
    past_k = cache_k[page_table].reshape(Bs, past_len, H_B, HD_B)
    past_v = cache_v[page_table].reshape(Bs, past_len, H_B, HD_B)
    y_sample, k_sample, v_sample, conv_sample, C_sample, n_sample, m_sample = run_trunk(
        x_sample, cache_mem_k, cache_mem_v, state_conv, state_C, state_n, state_m, past_k, past_v,
        g_ffn1, w_ffn1_up, w_ffn1_down, g_mix, w_in_a, b_gate_a, conv_w_a, conv_b_a, g_h_a,
        g_kv, w_kv, g_k_b, w_in_b, g_q_b, b_sb, g_q_mem, w_out, g_ffn2, w_ffn2_up, w_ffn2_down)
    return (y_prompt, y_sample, k_prompt, v_prompt, k_sample, v_sample, mem_k_prompt, mem_v_prompt,
            conv_prompt, C_prompt, n_prompt, m_prompt, conv_sample, C_sample, n_sample, m_sample)
```

```python
import functools

import jax
import jax.numpy as jnp
from jax import lax
from jax.experimental import pallas as pl
from jax.experimental.pallas import tpu as pltpu

F32 = jnp.float32
BF16 = jnp.bfloat16

EPS = 1e-6
D_MODEL = 1024
H_MEM, HD_MEM = 4, 64
D_MEM = H_MEM * HD_MEM
D_MAIN = D_MODEL - D_MEM
H_A, HD_A = 4, 192
H_B, HD_B = 12, 64
HD_A_PAD = 256
D_A_PAD = H_A * HD_A_PAD
CONV_W = 4
D_FF = 2816
FF_CHUNK = 256
LANES = 128
GROUP = 256
ROW_TILE = 1024
VMEM_LIMIT = 56 * 1024 * 1024


def _cparams(sem, vmem=VMEM_LIMIT):
    return pltpu.CompilerParams(dimension_semantics=sem, vmem_limit_bytes=vmem)


def _rms(x, g):
    return x * lax.rsqrt(jnp.mean(x * x, axis=-1, keepdims=True) + EPS) * g


def _sigmoid(x):
    return 1.0 / (1.0 + jnp.exp(-x))


def _softplus(x):
    return jnp.maximum(x, 0.0) + jnp.log(1.0 + jnp.exp(-jnp.abs(x)))


def _head_group_matrix(n, width):
    r = lax.broadcasted_iota(jnp.int32, (n, n), 0) // width
    c = lax.broadcasted_iota(jnp.int32, (n, n), 1) // width
    return (r == c).astype(BF16)


def _ffn_kernel(x_ref, g_ref, wup_ref, wdn_ref, o_ref, h_sc, acc_sc):
    c = pl.program_id(1)

    @pl.when(c == 0)
    def _():
        h_sc[...] = _rms(x_ref[...], g_ref[...]).astype(BF16)
        acc_sc[...] = jnp.zeros_like(acc_sc)

    gu = jnp.dot(h_sc[...], wup_ref[0], preferred_element_type=F32)
    g = gu[:, :FF_CHUNK]
    u = gu[:, FF_CHUNK:]
    act = (g * _sigmoid(g) * u).astype(BF16)
    acc_sc[...] += jnp.dot(act, wdn_ref[...], preferred_element_type=F32)

    @pl.when(c == pl.num_programs(1) - 1)
    def _():
        o_ref[...] = x_ref[...] + 0.5 * acc_sc[...]


def _ffn(x, g, wup_r, wdn):
    t, d = x.shape
    nchunk = wup_r.shape[0]
    tm = ROW_TILE
    return pl.pallas_call(
        _ffn_kernel,
        grid=(t // tm, nchunk),
        in_specs=[
            pl.BlockSpec((tm, d), lambda i, c: (i, 0)),
            pl.BlockSpec((1, d), lambda i, c: (0, 0)),
            pl.BlockSpec((1, d, 2 * FF_CHUNK), lambda i, c: (c, 0, 0)),
            pl.BlockSpec((FF_CHUNK, d), lambda i, c: (c, 0)),
        ],
        out_specs=pl.BlockSpec((tm, d), lambda i, c: (i, 0)),
        out_shape=jax.ShapeDtypeStruct((t, d), F32),
        scratch_shapes=[pltpu.VMEM((tm, d), BF16), pltpu.VMEM((tm, d), F32)],
        compiler_params=_cparams(("parallel", "arbitrary")),
        name="ffn",
    )(x, g.reshape(1, d), wup_r, wdn)


def _proj_kernel(x_ref, g_ref, w_ref, gain_ref, o_ref, h_sc, *, normed, head_width):
    j = pl.program_id(1)

    @pl.when(j == 0)
    def _():
        h_sc[...] = _rms(x_ref[...], g_ref[...]).astype(BF16)

    o_ref[...] = jnp.dot(h_sc[...], w_ref[...], preferred_element_type=F32)

    for tile, k in normed:
        @pl.when(j == tile)
        def _(k=k):
            sl = slice(k * GROUP, (k + 1) * GROUP)
            y = o_ref[:, sl]
            ss = jnp.dot((y * y).astype(BF16), _head_group_matrix(GROUP, head_width),
                         preferred_element_type=F32)
            o_ref[:, sl] = y * lax.rsqrt(ss * (1.0 / head_width) + EPS) * gain_ref[:, sl]


def _proj(x, g, w, gain, normed_groups, tn, tm=ROW_TILE, head_width=HD_B):
    t, d = x.shape
    n = w.shape[1]
    nsub = tn // GROUP
    normed = tuple((gi // nsub, gi % nsub) for gi in normed_groups)
    return pl.pallas_call(
        functools.partial(_proj_kernel, normed=normed, head_width=head_width),
        grid=(t // tm, n // tn),
        in_specs=[
            pl.BlockSpec((tm, d), lambda i, j: (i, 0)),
            pl.BlockSpec((1, d), lambda i, j: (0, 0)),
            pl.BlockSpec((d, tn), lambda i, j: (0, j)),
            pl.BlockSpec((1, tn), lambda i, j: (0, j)),
        ],
        out_specs=pl.BlockSpec((tm, tn), lambda i, j: (i, j)),
        out_shape=jax.ShapeDtypeStruct((t, n), F32),
        scratch_shapes=[pltpu.VMEM((tm, d), BF16)],
        compiler_params=_cparams(("parallel", "arbitrary")),
        name="proj",
    )(x, g.reshape(1, d), w, gain.reshape(1, n))


def _outproj_kernel(x_ref, a_ref, wa_ref, b_ref, wb_ref, o_ref):
    o_ref[...] = (x_ref[...]
                  + jnp.dot(a_ref[...], wa_ref[...], preferred_element_type=F32)
                  + jnp.dot(b_ref[...], wb_ref[...], preferred_element_type=F32))


def _outproj(x, a, wa, b, wb):
    t, d = x.shape
    tm = ROW_TILE
    ka, kb = a.shape[1], b.shape[1]
    return pl.pallas_call(
        _outproj_kernel,
        grid=(t // tm,),
        in_specs=[
            pl.BlockSpec((tm, d), lambda i: (i, 0)),
            pl.BlockSpec((tm, ka), lambda i: (i, 0)),
            pl.BlockSpec((ka, d), lambda i: (0, 0)),
            pl.BlockSpec((tm, kb), lambda i: (i, 0)),
            pl.BlockSpec((kb, d), lambda i: (0, 0)),
        ],
        out_specs=pl.BlockSpec((tm, d), lambda i: (i, 0)),
        out_shape=jax.ShapeDtypeStruct((t, d), F32),
        compiler_params=_cparams(("parallel",)),
        name="outproj",
    )(x, a, wa, b, wb)


def _memattn_kernel(q_ref, k_ref, v_ref, o_ref):
    q = q_ref[...]
    kb = k_ref[...].astype(BF16)
    vb = v_ref[...].astype(BF16)
    lane = lax.broadcasted_iota(jnp.int32, q.shape, 1) // HD_MEM
    acc = jnp.zeros(q.shape, F32)
    for h in range(H_MEM):
        qh = jnp.where(lane == h, q, 0.0).astype(BF16)
        s = lax.dot_general(qh, kb, (((1,), (1,)), ((), ())), preferred_element_type=F32) * (HD_MEM ** -0.5)
        e = jnp.exp(s - jnp.max(s, axis=-1, keepdims=True))
        p = (e / jnp.sum(e, axis=-1, keepdims=True)).astype(BF16)
        acc = acc + jnp.where(lane == h, jnp.dot(p, vb, preferred_element_type=F32), 0.0)
    o_ref[...] = acc.astype(o_ref.dtype)


def _memattn(qarr, qcol, row0, nb, lq, tq, karr, kcol, varr, vcol, out_dtype):
    m = karr.shape[0] // nb
    nq = lq // tq
    rb0 = row0 // tq
    return pl.pallas_call(
        _memattn_kernel,
        grid=(nb, nq),
        in_specs=[
            pl.BlockSpec((tq, D_MEM), lambda b, i: (rb0 + b * nq + i, qcol)),
            pl.BlockSpec((m, D_MEM), lambda b, i: (b, kcol)),
            pl.BlockSpec((m, D_MEM), lambda b, i: (b, vcol)),
        ],
        out_specs=pl.BlockSpec((tq, D_MEM), lambda b, i: (b * nq + i, 0)),
        out_shape=jax.ShapeDtypeStruct((nb * lq, D_MEM), out_dtype),
        compiler_params=_cparams(("parallel", "parallel")),
        name="memattn",
    )(qarr, karr, varr)


def _mlstm_kernel(qkvo_ref, gates_ref, convprev_ref, c0_ref, n0_ref, m0_ref,
                  convw_ref, convb_ref, bgate_ref, gh_ref,
                  h_ref, c_out_ref, n_out_ref, m_out_ref,
                  ext_sc, c_sc, n_sc, m_sc, *stage, lb, lp):
    ci = pl.program_id(1)
    last = lb - 1

    @pl.when(ci == 0)
    def _():
        ext_sc[0:8, :] = convprev_ref[0]
        c_sc[...] = jnp.zeros_like(c_sc)
        n_sc[...] = jnp.zeros_like(n_sc)
        for h in range(H_A):
            c_sc[h, 0:HD_A, 0:HD_A] = c0_ref[0, h]
        n_sc[0:H_A, 0:HD_A] = n0_ref[0]
        m_sc[0:H_A, :] = m0_ref[0]

    if lb == lp:
        src, gsrc = qkvo_ref, gates_ref
    else:
        src, gsrc = stage
        src[...] = jnp.zeros_like(src)
        gsrc[...] = jnp.zeros_like(gsrc)
        src[0:lb, :] = qkvo_ref[...]
        gsrc[0:lb, :] = gates_ref[...]

    ext_sc[8:8 + lp, :] = src[:, 0:2 * D_A_PAD]

    row = lax.broadcasted_iota(jnp.int32, (lp, lp), 0)
    col = lax.broadcasted_iota(jnp.int32, (lp, lp), 1)
    causal = row >= col
    valid_row = lax.broadcasted_iota(jnp.int32, (lp, 1), 0) < lb

    gl = lax.broadcasted_iota(jnp.int32, (lp, GROUP), 1)
    gts = gsrc[...] + bgate_ref[...]
    gts = jnp.where((gl >= H_A) & (gl < 2 * H_A), -_softplus(-gts), gts)
    gts = jnp.where(valid_row & (gl < 2 * H_A), gts, 0.0)
    cum = jnp.dot(causal.astype(F32), gts, preferred_element_type=F32, precision=lax.Precision.HIGHEST)
    gts_t = gts.T
    cum_t = cum.T

    def conv_silu(c0):
        sl = slice(c0, c0 + HD_A_PAD)
        a = convb_ref[:, sl]
        for j in range(CONV_W):
            a = a + ext_sc[pl.ds(8 - (CONV_W - 1) + j, lp), sl] * convw_ref[j:j + 1, sl]
        return jnp.where(valid_row, a * _sigmoid(a), 0.0)

    for h in range(H_A):
        hs = slice(h * HD_A_PAD, (h + 1) * HD_A_PAD)
        q = conv_silu(h * HD_A_PAD)
        k = conv_silu(D_A_PAD + h * HD_A_PAD) * (HD_A ** -0.5)
        v = src[:, 2 * D_A_PAD + h * HD_A_PAD:2 * D_A_PAD + (h + 1) * HD_A_PAD]
        o_pre = src[:, 3 * D_A_PAD + h * HD_A_PAD:3 * D_A_PAD + (h + 1) * HD_A_PAD]
        qb, kb, vb = q.astype(BF16), k.astype(BF16), v.astype(BF16)

        ig_col = gts[:, h:h + 1]
        b_col = cum[:, H_A + h:H_A + h + 1]
        ig_row = gts_t[h:h + 1, :]
        b_row = cum_t[H_A + h:H_A + h + 1, :]
        m_prev = m_sc[h:h + 1, 0:1]
        c_prev = c_sc[h]
        n_prev = n_sc[h:h + 1, :]

        dlog = jnp.where(causal, b_col - b_row + ig_row, -jnp.inf)
        inter = b_col + m_prev
        m_t = jnp.maximum(inter, jnp.max(dlog, axis=-1, keepdims=True))
        s = lax.dot_general(qb, kb, (((1,), (1,)), ((), ())), preferred_element_type=F32)
        w = jnp.exp(dlog - m_t) * s
        w_inter = jnp.exp(inter - m_t)
        cq = lax.dot_general(qb, c_prev.astype(BF16), (((1,), (1,)), ((), ())),
                             preferred_element_type=F32) * w_inter
        num = jnp.dot(w.astype(BF16), vb, preferred_element_type=F32) + cq
        den = jnp.sum(w, axis=-1, keepdims=True) + w_inter * jnp.sum(q * n_prev, axis=-1, keepdims=True)
        den = jnp.maximum(jnp.abs(den), jnp.exp(-m_t))
        hh = num / den
        hn = hh * lax.rsqrt(jnp.sum(hh * hh, axis=-1, keepdims=True) * (1.0 / HD_A) + EPS) * gh_ref[:, hs]
        out = hn * _sigmoid(o_pre)
        h_ref[:, hs] = out[0:lb, :].astype(h_ref.dtype)

        m_new = m_t[last:last + 1, :]
        b_last = b_col[last:last + 1, :]
        w_end = jnp.exp(b_last - b_col + ig_col - m_new)
        decay = jnp.exp(b_last + m_prev - m_new)
        kw = k * w_end
        c_sc[h] = decay * c_prev + lax.dot_general(vb, kw.astype(BF16), (((0,), (0,)), ((), ())),
                                                   preferred_element_type=F32)
        n_sc[h:h + 1, :] = decay * n_prev + jnp.sum(kw, axis=0, keepdims=True)
        m_sc[h:h + 1, :] = jnp.broadcast_to(m_new, (1, LANES))

    ext_sc[0:8, :] = ext_sc[lp:lp + 8, :]

    @pl.when(ci == pl.num_programs(1) - 1)
    def _():
        for h in range(H_A):
            c_out_ref[0, h] = c_sc[h, 0:HD_A, 0:HD_A]
        n_out_ref[0] = n_sc[0:H_A, 0:HD_A]
        m_out_ref[0] = m_sc[0:H_A, :]


def _mlstm(proj, row0, nb, seq, lb, lp, convprev, c0, n0, m0, convw, convb, bgate, gh, out_dtype):
    nc = seq // lb
    rb0 = row0 // lb
    gate_col = (4 * D_A_PAD + D_MEM) // GROUP
    scratch = [
        pltpu.VMEM((lp + 8, 2 * D_A_PAD), F32),
        pltpu.VMEM((H_A, HD_A_PAD, HD_A_PAD), F32),
        pltpu.VMEM((8, HD_A_PAD), F32),
        pltpu.VMEM((8, LANES), F32),
    ]
    if lb != lp:
        scratch += [pltpu.VMEM((lp, 4 * D_A_PAD), F32), pltpu.VMEM((lp, GROUP), F32)]
    return pl.pallas_call(
        functools.partial(_mlstm_kernel, lb=lb, lp=lp),
        grid=(nb, nc),
        in_specs=[
            pl.BlockSpec((lb, 4 * D_A_PAD), lambda b, c: (rb0 + b * nc + c, 0)),
            pl.BlockSpec((lb, GROUP), lambda b, c: (rb0 + b * nc + c, gate_col)),
            pl.BlockSpec((1, 8, 2 * D_A_PAD), lambda b, c: (b, 0, 0)),
            pl.BlockSpec((1, H_A, HD_A, HD_A), lambda b, c: (b, 0, 0, 0)),
            pl.BlockSpec((1, H_A, HD_A), lambda b, c: (b, 0, 0)),
            pl.BlockSpec((1, H_A, LANES), lambda b, c: (b, 0, 0)),
            pl.BlockSpec((CONV_W, 2 * D_A_PAD), lambda b, c: (0, 0)),
            pl.BlockSpec((1, 2 * D_A_PAD), lambda b, c: (0, 0)),
            pl.BlockSpec((1, GROUP), lambda b, c: (0, 0)),
            pl.BlockSpec((1, D_A_PAD), lambda b, c: (0, 0)),
        ],
        out_specs=[
            pl.BlockSpec((lb, D_A_PAD), lambda b, c: (b * nc + c, 0)),
            pl.BlockSpec((1, H_A, HD_A, HD_A), lambda b, c: (b, 0, 0, 0)),
            pl.BlockSpec((1, H_A, HD_A), lambda b, c: (b, 0, 0)),
            pl.BlockSpec((1, H_A, LANES), lambda b, c: (b, 0, 0)),
        ],
        out_shape=[
            jax.ShapeDtypeStruct((nb * seq, D_A_PAD), out_dtype),
            jax.ShapeDtypeStruct((nb, H_A, HD_A, HD_A), F32),
            jax.ShapeDtypeStruct((nb, H_A, HD_A), F32),
            jax.ShapeDtypeStruct((nb, H_A, LANES), F32),
        ],
        scratch_shapes=scratch,
        compiler_params=_cparams(("parallel", "arbitrary")),
        name="mlstm",
    )(proj, proj, convprev, c0, n0, m0, convw, convb, bgate, gh)


SB_BLOCK = 256


def _strict_upper(n):
    r = lax.broadcasted_iota(jnp.int32, (n, n), 0)
    c = lax.broadcasted_iota(jnp.int32, (n, n), 1)
    return (r > c).astype(BF16)


def _sb_block(z, mask, later_mat, carry):
    sp = _softplus(z)
    lk = -sp if mask is None else jnp.where(mask, -sp, 0.0)
    later = jnp.dot(lk.astype(BF16), later_mat, preferred_element_type=F32) + carry
    a = jnp.exp(z - sp + later)
    if mask is not None:
        a = jnp.where(mask, a, 0.0)
    return a, later[:, 0:1] + lk[:, 0:1]


def _sb_prompt_kernel(bias_ref, q_ref, k_ref, v_ref, o_ref, kb_sc, vb_sc):
    p = pl.program_id(1)
    i = pl.program_id(2)
    blk = SB_BLOCK

    @pl.when(i == 0)
    def _():
        kb_sc[...] = k_ref[...].astype(BF16)
        vb_sc[...] = v_ref[...].astype(BF16)

    q = q_ref[...]
    lane = lax.broadcasted_iota(jnp.int32, q.shape, 1) // HD_B
    later_mat = _strict_upper(blk)
    rel = (lax.broadcasted_iota(jnp.int32, (blk, blk), 1)
           - lax.broadcasted_iota(jnp.int32, (blk, blk), 0))
    out = jnp.zeros(q.shape, F32)
    for hh in range(LANES // HD_B):
        qh = (jnp.where(lane == hh, q, 0.0) * (HD_B ** -0.5)).astype(BF16)
        bias = bias_ref[p * (LANES // HD_B) + hh]

        def body(jj, state, qh=qh, bias=bias):
            carry, acc = state
            j = i - jj
            off = pl.multiple_of(j * blk, blk)
            kb = kb_sc[pl.ds(off, blk), :]
            vb = vb_sc[pl.ds(off, blk), :]
            z = lax.dot_general(qh, kb, (((1,), (1,)), ((), ())), preferred_element_type=F32) + bias
            mask = rel < (i - j) * blk
            a, carry = _sb_block(z, mask, later_mat, carry)
            acc = acc + jnp.dot(a.astype(BF16), vb, preferred_element_type=F32)
            return carry, acc

        _, acc = lax.fori_loop(0, i + 1, body, (jnp.zeros((blk, 1), F32), jnp.zeros(q.shape, F32)))
        out = out + jnp.where(lane == hh, acc, 0.0)
    o_ref[...] = out.astype(o_ref.dtype)


def _sb_prompt(qarr, kvarr, bias, nb, seq):
    blk = SB_BLOCK
    npair = D_MAIN // LANES
    ni = seq // blk
    grid_spec = pltpu.PrefetchScalarGridSpec(
        num_scalar_prefetch=1,
        grid=(nb, npair, ni),
        in_specs=[
            pl.BlockSpec((blk, LANES), lambda b, p, i, s: (b * ni + i, p)),
            pl.BlockSpec((seq, LANES), lambda b, p, i, s: (b, p)),
            pl.BlockSpec((seq, LANES), lambda b, p, i, s: (b, npair + p)),
        ],
        out_specs=pl.BlockSpec((blk, LANES), lambda b, p, i, s: (b * ni + i, p)),
        scratch_shapes=[pltpu.VMEM((seq, LANES), BF16), pltpu.VMEM((seq, LANES), BF16)],
    )
    return pl.pallas_call(
        _sb_prompt_kernel,
        grid_spec=grid_spec,
        out_shape=jax.ShapeDtypeStruct((nb * seq, D_MAIN), BF16),
        compiler_params=_cparams(("parallel", "parallel", "arbitrary")),
        name="sb_prompt",
    )(bias, qarr, kvarr, kvarr)


def _sb_sample_kernel(pt_ref, q_ref, knew_ref, vnew_ref, kc_ref, vc_ref, bias_ref, o_ref,
                      q_sc, carry_sc, acc_sc, pad_sc, *, lq, page):
    g = pl.program_id(1)
    rows = H_B * lq
    later_mat = _strict_upper(page)

    def step(kb, vb, mask):
        z = lax.dot_general(q_sc[...].astype(BF16), kb, (((1,), (1,)), ((), ())),
                            preferred_element_type=F32) + bias_ref[...]
        a, carry = _sb_block(z, mask, later_mat, carry_sc[:, 0:1])
        acc_sc[...] += jnp.dot(a.astype(BF16), vb, preferred_element_type=F32)
        carry_sc[...] = jnp.broadcast_to(carry, carry_sc.shape)

    @pl.when(g == 0)
    def _():
        q = q_ref[...]
        lane = lax.broadcasted_iota(jnp.int32, q.shape, 1) // HD_B
        for h in range(H_B):
            q_sc[h * lq:(h + 1) * lq, :] = jnp.where(lane == h, q, 0.0) * (HD_B ** -0.5)
        carry_sc[...] = jnp.zeros_like(carry_sc)
        acc_sc[...] = jnp.zeros_like(acc_sc)
        kidx = lax.broadcasted_iota(jnp.int32, (rows, page), 1)
        qidx = lax.broadcasted_iota(jnp.int32, (rows, page), 0) % lq
        pad_sc[...] = jnp.zeros_like(pad_sc)
        pad_sc[0:lq, :] = knew_ref[...]
        kb = pad_sc[...].astype(BF16)
        pad_sc[0:lq, :] = vnew_ref[...]
        vb = pad_sc[...].astype(BF16)
        step(kb, vb, kidx < qidx)

    @pl.when(g > 0)
    def _():
        step(kc_ref[0].astype(BF16), vc_ref[0].astype(BF16), None)

    @pl.when(g == pl.num_programs(1) - 1)
    def _():
        lane = lax.broadcasted_iota(jnp.int32, (lq, D_MAIN), 1) // HD_B
        out = jnp.zeros((lq, D_MAIN), F32)
        for h in range(H_B):
            out = out + jnp.where(lane == h, acc_sc[h * lq:(h + 1) * lq, :], 0.0)
        o_ref[...] = out


def _sb_sample(qarr, kvarr, row0, nb, lq, cache_k, cache_v, page_table, bias_rows):
    npage = page_table.shape[1]
    page = cache_k.shape[1]
    rb0 = row0 // lq
    rows = H_B * lq

    def page_map(b, g, pt):
        return (pt[b, npage - jnp.maximum(g, 1)], 0, 0)

    grid_spec = pltpu.PrefetchScalarGridSpec(
        num_scalar_prefetch=1,
        grid=(nb, npage + 1),
        in_specs=[
            pl.BlockSpec((lq, D_MAIN), lambda b, g, pt: (rb0 + b, 0)),
            pl.BlockSpec((lq, D_MAIN), lambda b, g, pt: (rb0 + b, 0)),
            pl.BlockSpec((lq, D_MAIN), lambda b, g, pt: (rb0 + b, 1)),
            pl.BlockSpec((1, page, D_MAIN), page_map),
            pl.BlockSpec((1, page, D_MAIN), page_map),
            pl.BlockSpec((rows, page), lambda b, g, pt: (0, 0)),
        ],
        out_specs=pl.BlockSpec((lq, D_MAIN), lambda b, g, pt: (b, 0)),
        scratch_shapes=[
            pltpu.VMEM((rows, D_MAIN), F32),
            pltpu.VMEM((rows, LANES), F32),
            pltpu.VMEM((rows, D_MAIN), F32),
            pltpu.VMEM((page, D_MAIN), F32),
        ],
    )
    return pl.pallas_call(
        functools.partial(_sb_sample_kernel, lq=lq, page=page),
        grid_spec=grid_spec,
        out_shape=jax.ShapeDtypeStruct((nb * lq, D_MAIN), F32),
        compiler_params=_cparams(("parallel", "arbitrary")),
        name="sb_sample",
    )(page_table, qarr, kvarr, kvarr, cache_k, cache_v, bias_rows)


def _pad_heads(w):
    lead = w.shape[:-1]
    w = w.reshape(lead + (H_A, HD_A))
    w = jnp.pad(w, [(0, 0)] * len(lead) + [(0, 0), (0, HD_A_PAD - HD_A)])
    return w.reshape(lead + (D_A_PAD,))


def _unpad_heads(w, nh):
    lead = w.shape[:-1]
    return w.reshape(lead + (nh, HD_A_PAD))[..., :HD_A].reshape(lead + (nh * HD_A,))


def _ffn_weights(w_up, w_down):
    nchunk = D_FF // FF_CHUNK
    wg = w_up[:, :D_FF].reshape(D_MODEL, nchunk, FF_CHUNK)
    wu = w_up[:, D_FF:].reshape(D_MODEL, nchunk, FF_CHUNK)
    wup_r = jnp.concatenate([wg, wu], axis=-1).transpose(1, 0, 2).astype(BF16)
    return wup_r, w_down.astype(BF16)


def kernel(x_prompt, x_sample, cache_k, cache_v, cache_mem_k, cache_mem_v, state_C, state_n, state_m, state_conv, page_table, mem_prompt, g_ffn1, w_ffn1_up, w_ffn1_down, g_mix, w_in_a, b_gate_a, conv_w_a, conv_b_a, g_h_a, g_kv, w_kv, g_k_b, w_in_b, g_q_b, b_sb, g_mem, w_mem_kv, g_q_mem, g_k_mem, w_out, g_ffn2, w_ffn2_up, w_ffn2_down):
    bp, seq, d = x_prompt.shape
    bs, lq, _ = x_sample.shape
    depth = g_ffn1.shape[0]
    n_a = w_in_a.shape[0]
    tp = bp * seq
    mem_len = mem_prompt.shape[1]
    n_phys, page = cache_k.shape[0], cache_k.shape[1]

    x = jnp.concatenate([x_prompt.reshape(tp, d), x_sample.reshape(bs * lq, d)], axis=0)
    ones_d = jnp.ones((d,), F32)

    mem_rows = mem_prompt.reshape(bp * mem_len, d)
    mem_kv_p = []
    for l in range(depth):
        gain = jnp.concatenate([jnp.tile(g_k_mem[l], H_MEM), jnp.ones((D_MEM,), F32)])
        mem_kv_p.append(_proj(mem_rows, g_mem[l], w_mem_kv[l].astype(BF16), gain, (0,), tn=2 * D_MEM,
                              tm=bp * mem_len, head_width=HD_MEM))
    cmk = cache_mem_k.reshape(depth, bs * mem_len, D_MEM)
    cmv = cache_mem_v.reshape(depth, bs * mem_len, D_MEM)

    ck = cache_k.reshape(n_phys, page, D_MAIN)
    cv = cache_v.reshape(n_phys, page, D_MAIN)

    convs_p, convs_s, cs_p, cs_s, ns_p, ns_s, ms_p, ms_s = [], [], [], [], [], [], [], []
    kv = None
    for l in range(depth):
        if l == n_a:
            gain = jnp.concatenate([jnp.tile(g_k_b, H_B), jnp.ones((D_MAIN,), F32)])
            kv = _proj(x, g_kv, w_kv.astype(BF16), gain, tuple(range(D_MAIN // GROUP)), tn=D_MAIN)
        x = _ffn(x, g_ffn1[l], *_ffn_weights(w_ffn1_up[l], w_ffn1_down[l]))
        wo = w_out[l]
        if l < n_a:
            wi = w_in_a[l]
            q_w, k_w = wi[:, :D_MAIN], wi[:, D_MAIN:2 * D_MAIN]
            v_w, o_w = wi[:, 2 * D_MAIN:3 * D_MAIN], wi[:, 3 * D_MAIN:4 * D_MAIN]
            gate_w = wi[:, 4 * D_MAIN:4 * D_MAIN + 2 * H_A]
            qm_w = wi[:, 4 * D_MAIN + 2 * H_A:]
            w_a = jnp.concatenate([_pad_heads(q_w), _pad_heads(k_w), _pad_heads(v_w), _pad_heads(o_w), qm_w,
                                   jnp.pad(gate_w, ((0, 0), (0, GROUP - 2 * H_A)))], axis=1).astype(BF16)
            n_cols = w_a.shape[1]
            gain = jnp.concatenate([jnp.ones((4 * D_A_PAD,), F32), jnp.tile(g_q_mem[l], H_MEM),
                                    jnp.ones((GROUP,), F32)])
            proj = _proj(x, g_mix[l], w_a, gain, (4 * D_A_PAD // GROUP,), tn=2 * GROUP, head_width=HD_MEM)
            qm_col = 4 * D_A_PAD // D_MEM

            convw = jnp.concatenate([_pad_heads(conv_w_a[l][:, :D_MAIN]), _pad_heads(conv_w_a[l][:, D_MAIN:])], axis=1)
            convb = jnp.concatenate([_pad_heads(conv_b_a[l][:D_MAIN]), _pad_heads(conv_b_a[l][D_MAIN:])]).reshape(1, -1)
            bgate = jnp.pad(b_gate_a[l], (0, GROUP - 2 * H_A)).reshape(1, GROUP)
            gh = _pad_heads(g_h_a[l].reshape(-1)).reshape(1, D_A_PAD)

            def conv_state(c):
                c = jnp.concatenate([_pad_heads(c[..., :D_MAIN]), _pad_heads(c[..., D_MAIN:])], axis=-1)
                return jnp.pad(c, ((0, 0), (8 - (CONV_W - 1), 0), (0, 0)))

            hp, c_p, n_p, m_p = _mlstm(
                proj, 0, bp, seq, SB_BLOCK, SB_BLOCK,
                conv_state(jnp.zeros((bp, CONV_W - 1, 2 * D_MAIN), F32)),
                jnp.zeros((bp, H_A, HD_A, HD_A), F32), jnp.zeros((bp, H_A, HD_A), F32),
                jnp.zeros((bp, H_A, LANES), F32), convw, convb, bgate, gh, BF16)
            hs, c_s, n_s, m_s = _mlstm(
                proj, tp, bs, lq, lq, LANES, conv_state(state_conv[l]), state_C[l], state_n[l],
                jnp.broadcast_to(state_m[l][..., None], (bs, H_A, LANES)), convw, convb, bgate, gh, F32)
            mix_main = jnp.concatenate([hp, hs.astype(BF16)], axis=0)
            w_main = _pad_heads(wo[:D_MAIN].T).T.astype(BF16)

            qk_pre = proj[:, :2 * D_A_PAD]
            conv_rows_p = qk_pre[:tp].reshape(bp, seq, 2 * D_A_PAD)[:, seq - (CONV_W - 1):]
            conv_rows_s = qk_pre[tp:].reshape(bs, lq, 2 * D_A_PAD)[:, lq - (CONV_W - 1):]
            convs_p.append(_unpad_heads(conv_rows_p, 2 * H_A))
            convs_s.append(_unpad_heads(conv_rows_s, 2 * H_A))
            cs_p.append(c_p); ns_p.append(n_p); ms_p.append(m_p[..., 0])
            cs_s.append(c_s); ns_s.append(n_s); ms_s.append(m_s[..., 0])
        else:
            j = l - n_a
            gain = jnp.concatenate([jnp.tile(g_q_b[j], H_B), jnp.tile(g_q_mem[l], H_MEM)])
            proj = _proj(x, g_mix[l], w_in_b[j].astype(BF16), gain, tuple(range(D_MODEL // GROUP)), tn=2 * GROUP)
            qm_col = D_MAIN // D_MEM
            sp = _sb_prompt(proj, kv, b_sb[j], bp, seq)
            bias_rows = jnp.broadcast_to(jnp.repeat(b_sb[j], lq)[:, None], (H_B * lq, page))
            ss = _sb_sample(proj, kv, tp, bs, lq, ck, cv, page_table, bias_rows)
            mix_main = jnp.concatenate([sp, ss.astype(BF16)], axis=0)
            w_main = wo[:D_MAIN].astype(BF16)

        mem_p = _memattn(proj, qm_col, 0, bp, seq, 512, mem_kv_p[l], 0, mem_kv_p[l], 1, BF16)
        mem_s = _memattn(proj, qm_col, tp, bs, lq, lq, cmk[l], 0, cmv[l], 0, F32)
        mix_mem = jnp.concatenate([mem_p, mem_s.astype(BF16)], axis=0)
        x = _outproj(x, mix_main, w_main, mix_mem, wo[D_MAIN:].astype(BF16))
        x = _ffn(x, g_ffn2[l], *_ffn_weights(w_ffn2_up[l], w_ffn2_down[l]))

    mem_k_prompt = jnp.stack([m[:, :D_MEM] for m in mem_kv_p]).reshape(depth, bp, mem_len, H_MEM, HD_MEM)
    mem_v_prompt = jnp.stack([m[:, D_MEM:] for m in mem_kv_p]).reshape(depth, bp, mem_len, H_MEM, HD_MEM)
    return (x[:tp].reshape(bp, seq, d), x[tp:].reshape(bs, lq, d),
            kv[:tp, :D_MAIN].reshape(bp, seq, H_B, HD_B), kv[:tp, D_MAIN:].reshape(bp, seq, H_B, HD_B),
            kv[tp:, :D_MAIN].reshape(bs, lq, H_B, HD_B), kv[tp:, D_MAIN:].reshape(bs, lq, H_B, HD_B),
            mem_k_prompt, mem_v_prompt,
            jnp.stack(convs_p), jnp.stack(cs_p), jnp.stack(ns_p), jnp.stack(ms_p),
            jnp.stack(convs_s), jnp.stack(cs_s), jnp.stack(ns_s), jnp.stack(ms_s))
```

```python
import functools

import jax
import jax.numpy as jnp
from jax import lax
from jax.experimental import pallas as pl
from jax.experimental.pallas import tpu as pltpu

F32 = jnp.float32
BF16 = jnp.bfloat16

EPS = 1e-6
D_MODEL = 1024
H_MEM, HD_MEM = 4, 64
D_MEM = H_MEM * HD_MEM
D_MAIN = D_MODEL - D_MEM
H_A, HD_A = 4, 192
H_B, HD_B = 12, 64
HD_A_PAD = 256
D_A_PAD = H_A * HD_A_PAD
CONV_W = 4
D_FF = 2816
FF_CHUNK = 256
LANES = 128
GROUP = 256
ROW_TILE = 1024
FFN_ROW_TILE = 1024
VMEM_LIMIT = 56 * 1024 * 1024


def _cparams(sem, vmem=VMEM_LIMIT):
    return pltpu.CompilerParams(dimension_semantics=sem, vmem_limit_bytes=vmem)


def _rms(x, g):
    return x * lax.rsqrt(jnp.mean(x * x, axis=-1, keepdims=True) + EPS) * g


def _sigmoid(x):
    return 1.0 / (1.0 + jnp.exp(-x))


def _softplus(x):
    return jnp.maximum(x, 0.0) + jnp.log(1.0 + jnp.exp(-jnp.abs(x)))


def _head_group_matrix(n, width):
    r = lax.broadcasted_iota(jnp.int32, (n, n), 0) // width
    c = lax.broadcasted_iota(jnp.int32, (n, n), 1) // width
    return (r == c).astype(BF16)


def _ffn_kernel(x_ref, g_ref, wup_ref, wdn_ref, o_ref, h_sc, acc_sc):
    h_sc[...] = _rms(x_ref[...], g_ref[...]).astype(BF16)
    for c in range(wup_ref.shape[0]):
        gu = jnp.dot(h_sc[...], wup_ref[c], preferred_element_type=F32)
        g = gu[:, :FF_CHUNK]
        u = gu[:, FF_CHUNK:]
        act = (g * _sigmoid(g) * u).astype(BF16)
        down = jnp.dot(act, wdn_ref[c * FF_CHUNK:(c + 1) * FF_CHUNK, :], preferred_element_type=F32)
        if c == 0:
            acc_sc[...] = down
        else:
            acc_sc[...] += down
    o_ref[...] = x_ref[...] + 0.5 * acc_sc[...]


def _ffn(x, g, wup_r, wdn, row0=0, nrows=None):
    t, d = x.shape
    nrows = t if nrows is None else nrows
    nchunk = wup_r.shape[0]
    tm = FFN_ROW_TILE
    rb0 = row0 // tm
    resident = pl.Buffered(1)
    return pl.pallas_call(
        _ffn_kernel,
        grid=(nrows // tm,),
        in_specs=[
            pl.BlockSpec((tm, d), lambda i: (rb0 + i, 0)),
            pl.BlockSpec((1, d), lambda i: (0, 0)),
            pl.BlockSpec((nchunk, d, 2 * FF_CHUNK), lambda i: (0, 0, 0), pipeline_mode=resident),
            pl.BlockSpec((nchunk * FF_CHUNK, d), lambda i: (0, 0), pipeline_mode=resident),
        ],
        out_specs=pl.BlockSpec((tm, d), lambda i: (i, 0)),
        out_shape=jax.ShapeDtypeStruct((nrows, d), F32),
        scratch_shapes=[pltpu.VMEM((tm, d), BF16), pltpu.VMEM((tm, d), F32)],
        compiler_params=_cparams(("parallel",)),
        name="ffn",
    )(x, g.reshape(1, d), wup_r, wdn)


def _proj_kernel(x_ref, g_ref, w_ref, gain_ref, o_ref, h_sc, *, normed, head_width):
    j = pl.program_id(1)

    @pl.when(j == 0)
    def _():
        h_sc[...] = _rms(x_ref[...], g_ref[...]).astype(BF16)

    o_ref[...] = jnp.dot(h_sc[...], w_ref[...], preferred_element_type=F32)

    for tile, k in normed:
        @pl.when(j == tile)
        def _(k=k):
            sl = slice(k * GROUP, (k + 1) * GROUP)
            y = o_ref[:, sl]
            ss = jnp.dot((y * y).astype(BF16), _head_group_matrix(GROUP, head_width),
                         preferred_element_type=F32)
            o_ref[:, sl] = y * lax.rsqrt(ss * (1.0 / head_width) + EPS) * gain_ref[:, sl]


def _proj(x, g, w, gain, normed_groups, tn, tm=ROW_TILE, head_width=HD_B):
    t, d = x.shape
    n = w.shape[1]
    nsub = tn // GROUP
    normed = tuple((gi // nsub, gi % nsub) for gi in normed_groups)
    return pl.pallas_call(
        functools.partial(_proj_kernel, normed=normed, head_width=head_width),
        grid=(t // tm, n // tn),
        in_specs=[
            pl.BlockSpec((tm, d), lambda i, j: (i, 0)),
            pl.BlockSpec((1, d), lambda i, j: (0, 0)),
            pl.BlockSpec((d, tn), lambda i, j: (0, j)),
            pl.BlockSpec((1, tn), lambda i, j: (0, j)),
        ],
        out_specs=pl.BlockSpec((tm, tn), lambda i, j: (i, j)),
        out_shape=jax.ShapeDtypeStruct((t, n), F32),
        scratch_shapes=[pltpu.VMEM((tm, d), BF16)],
        compiler_params=_cparams(("parallel", "arbitrary")),
        name="proj",
    )(x, g.reshape(1, d), w, gain.reshape(1, n))


def _outproj_kernel(x_ref, a_ref, wa_ref, b_ref, wb_ref, o_ref):
    o_ref[...] = (x_ref[...]
                  + jnp.dot(a_ref[...].astype(BF16), wa_ref[...], preferred_element_type=F32)
                  + jnp.dot(b_ref[...].astype(BF16), wb_ref[...], preferred_element_type=F32))


def _outproj(x, row0, a, wa, b, wb):
    t, d = x.shape
    tm = ROW_TILE
    nrows, ka = a.shape
    kb = b.shape[1]
    rb0 = row0 // tm
    return pl.pallas_call(
        _outproj_kernel,
        grid=(nrows // tm,),
        in_specs=[
            pl.BlockSpec((tm, d), lambda i: (rb0 + i, 0)),
            pl.BlockSpec((tm, ka), lambda i: (i, 0)),
            pl.BlockSpec((ka, d), lambda i: (0, 0)),
            pl.BlockSpec((tm, kb), lambda i: (i, 0)),
            pl.BlockSpec((kb, d), lambda i: (0, 0)),
        ],
        out_specs=pl.BlockSpec((tm, d), lambda i: (rb0 + i, 0)),
        out_shape=jax.ShapeDtypeStruct((t, d), F32),
        input_output_aliases={0: 0},
        compiler_params=_cparams(("parallel",)),
        name="outproj",
    )(x, a, wa, b, wb)


def _memattn_kernel(q_ref, k_ref, v_ref, o_ref):
    q = q_ref[...]
    kb = k_ref[...].astype(BF16)
    vb = v_ref[...].astype(BF16)
    lane = lax.broadcasted_iota(jnp.int32, q.shape, 1) // HD_MEM
    acc = jnp.zeros(q.shape, F32)
    for h in range(H_MEM):
        qh = jnp.where(lane == h, q, 0.0).astype(BF16)
        s = lax.dot_general(qh, kb, (((1,), (1,)), ((), ())), preferred_element_type=F32) * (HD_MEM ** -0.5)
        e = jnp.exp(s - jnp.max(s, axis=-1, keepdims=True))
        p = (e / jnp.sum(e, axis=-1, keepdims=True)).astype(BF16)
        acc = acc + jnp.where(lane == h, jnp.dot(p, vb, preferred_element_type=F32), 0.0)
    o_ref[...] = acc.astype(o_ref.dtype)


def _memattn(qarr, qcol, row0, nb, lq, tq, karr, kcol, varr, vcol, out_dtype):
    m = karr.shape[0] // nb
    nq = lq // tq
    rb0 = row0 // tq
    return pl.pallas_call(
        _memattn_kernel,
        grid=(nb, nq),
        in_specs=[
            pl.BlockSpec((tq, D_MEM), lambda b, i: (rb0 + b * nq + i, qcol)),
            pl.BlockSpec((m, D_MEM), lambda b, i: (b, kcol)),
            pl.BlockSpec((m, D_MEM), lambda b, i: (b, vcol)),
        ],
        out_specs=pl.BlockSpec((tq, D_MEM), lambda b, i: (b * nq + i, 0)),
        out_shape=jax.ShapeDtypeStruct((nb * lq, D_MEM), out_dtype),
        compiler_params=_cparams(("parallel", "parallel")),
        name="memattn",
    )(qarr, karr, varr)


def _memattn_small_kernel(q_ref, k_ref, v_ref, o_ref, *, nseq, lq):
    lane = lax.broadcasted_iota(jnp.int32, (lq, D_MEM), 1) // HD_MEM
    scores = []
    for s in range(nseq):
        q = q_ref[s * lq:(s + 1) * lq, :]
        qx = jnp.concatenate([jnp.where(lane == h, q, 0.0) for h in range(H_MEM)], axis=0).astype(BF16)
        kt = k_ref[s * D_MEM:(s + 1) * D_MEM, :].astype(BF16)
        scores.append(jnp.dot(qx, kt, preferred_element_type=F32) * (HD_MEM ** -0.5))
    probs = []
    for sc in scores:
        e = jnp.exp(sc - jnp.max(sc, axis=-1, keepdims=True))
        probs.append((e / jnp.sum(e, axis=-1, keepdims=True)).astype(BF16))
    for s, p in enumerate(probs):
        vt = v_ref[s * D_MEM:(s + 1) * D_MEM, :].astype(BF16)
        ox = lax.dot_general(p, vt, (((1,), (1,)), ((), ())), preferred_element_type=F32)
        out = jnp.zeros((lq, D_MEM), F32)
        for h in range(H_MEM):
            out = out + jnp.where(lane == h, ox[h * lq:(h + 1) * lq, :], 0.0)
        o_ref[s * lq:(s + 1) * lq, :] = out


def _memattn_small(qarr, qcol, row0, nb, lq, m, ktarr, vtarr, seq0, nseq=8):
    rb0 = row0 // (nseq * lq)
    kb0 = seq0 // nseq
    return pl.pallas_call(
        functools.partial(_memattn_small_kernel, nseq=nseq, lq=lq),
        grid=(nb // nseq,),
        in_specs=[
            pl.BlockSpec((nseq * lq, D_MEM), lambda i: (rb0 + i, qcol)),
            pl.BlockSpec((nseq * D_MEM, m), lambda i: (kb0 + i, 0)),
            pl.BlockSpec((nseq * D_MEM, m), lambda i: (kb0 + i, 0)),
        ],
        out_specs=pl.BlockSpec((nseq * lq, D_MEM), lambda i: (i, 0)),
        out_shape=jax.ShapeDtypeStruct((nb * lq, D_MEM), F32),
        compiler_params=_cparams(("parallel",)),
        name="memattn_small",
    )(qarr, ktarr, vtarr)


def _mlstm_kernel(qkvo_ref, gates_ref, convprev_ref, c0_ref, n0_ref, m0_ref,
                  convw_ref, convb_ref, bgate_ref, gh_ref, c_all_ref,
                  h_ref, c_out_ref, n_out_ref, m_out_ref,
                  ext_sc, c_sc, n_sc, m_sc, *stage, lb, lp):
    ci = pl.program_id(1)
    last = lb - 1

    @pl.when(ci == 0)
    def _():
        ext_sc[0:8, :] = convprev_ref[0]
        c_sc[...] = jnp.zeros_like(c_sc)
        n_sc[...] = jnp.zeros_like(n_sc)
        for h in range(H_A):
            c_sc[h, 0:HD_A, 0:HD_A] = c0_ref[0, 0, h]
        n_sc[0:H_A, 0:HD_A] = n0_ref[0]
        m_sc[0:H_A, :] = m0_ref[0]

    if lb == lp:
        src, gsrc = qkvo_ref, gates_ref
    else:
        src, gsrc = stage
        src[...] = jnp.zeros_like(src)
        gsrc[...] = jnp.zeros_like(gsrc)
        src[0:lb, :] = qkvo_ref[...]
        gsrc[0:lb, :] = gates_ref[...]

    ext_sc[8:8 + lp, :] = src[:, 0:2 * D_A_PAD]

    row = lax.broadcasted_iota(jnp.int32, (lp, lp), 0)
    col = lax.broadcasted_iota(jnp.int32, (lp, lp), 1)
    causal = row >= col
    valid_row = lax.broadcasted_iota(jnp.int32, (lp, 1), 0) < lb

    gl = lax.broadcasted_iota(jnp.int32, (lp, GROUP), 1)
    gts = gsrc[...] + bgate_ref[...]
    gts = jnp.where((gl >= H_A) & (gl < 2 * H_A), -_softplus(-gts), gts)
    gts = jnp.where(valid_row & (gl < 2 * H_A), gts, 0.0)
    cum = jnp.dot(causal.astype(F32), gts, preferred_element_type=F32, precision=lax.Precision.HIGHEST)
    gts_t = gts.T
    cum_t = cum.T

    def conv_silu(c0):
        sl = slice(c0, c0 + HD_A_PAD)
        a = convb_ref[:, sl]
        for j in range(CONV_W):
            a = a + ext_sc[pl.ds(8 - (CONV_W - 1) + j, lp), sl] * convw_ref[j:j + 1, sl]
        return jnp.where(valid_row, a * _sigmoid(a), 0.0)

    for h in range(H_A):
        hs = slice(h * HD_A_PAD, (h + 1) * HD_A_PAD)
        q = conv_silu(h * HD_A_PAD)
        k = conv_silu(D_A_PAD + h * HD_A_PAD) * (HD_A ** -0.5)
        v = src[:, 2 * D_A_PAD + h * HD_A_PAD:2 * D_A_PAD + (h + 1) * HD_A_PAD]
        o_pre = src[:, 3 * D_A_PAD + h * HD_A_PAD:3 * D_A_PAD + (h + 1) * HD_A_PAD]
        qb, kb, vb = q.astype(BF16), k.astype(BF16), v.astype(BF16)

        ig_col = gts[:, h:h + 1]
        b_col = cum[:, H_A + h:H_A + h + 1]
        ig_row = gts_t[h:h + 1, :]
        b_row = cum_t[H_A + h:H_A + h + 1, :]
        m_prev = m_sc[h:h + 1, 0:1]
        c_prev = c_sc[h]
        n_prev = n_sc[h:h + 1, :]

        dlog = jnp.where(causal, b_col - b_row + ig_row, -jnp.inf)
        inter = b_col + m_prev
        m_t = jnp.maximum(inter, jnp.max(dlog, axis=-1, keepdims=True))
        s = lax.dot_general(qb, kb, (((1,), (1,)), ((), ())), preferred_element_type=F32)
        w = jnp.exp(dlog - m_t) * s
        w_inter = jnp.exp(inter - m_t)
        cq = lax.dot_general(qb, c_prev.astype(BF16), (((1,), (1,)), ((), ())),
                             preferred_element_type=F32) * w_inter
        num = jnp.dot(w.astype(BF16), vb, preferred_element_type=F32) + cq
        den = jnp.sum(w, axis=-1, keepdims=True) + w_inter * jnp.sum(q * n_prev, axis=-1, keepdims=True)
        den = jnp.maximum(jnp.abs(den), jnp.exp(-m_t))
        hh = num / den
        hn = hh * lax.rsqrt(jnp.sum(hh * hh, axis=-1, keepdims=True) * (1.0 / HD_A) + EPS) * gh_ref[:, hs]
        out = hn * _sigmoid(o_pre)
        h_ref[:, hs] = out[0:lb, :].astype(h_ref.dtype)

        m_new = m_t[last:last + 1, :]
        b_last = b_col[last:last + 1, :]
        w_end = jnp.exp(b_last - b_col + ig_col - m_new)
        decay = jnp.exp(b_last + m_prev - m_new)
        kw = k * w_end
        c_sc[h] = decay * c_prev + lax.dot_general(vb, kw.astype(BF16), (((0,), (0,)), ((), ())),
                                                   preferred_element_type=F32)
        n_sc[h:h + 1, :] = decay * n_prev + jnp.sum(kw, axis=0, keepdims=True)
        m_sc[h:h + 1, :] = jnp.broadcast_to(m_new, (1, LANES))

    ext_sc[0:8, :] = ext_sc[lp:lp + 8, :]

    @pl.when(ci == pl.num_programs(1) - 1)
    def _():
        for h in range(H_A):
            c_out_ref[0, 0, h] = c_sc[h, 0:HD_A, 0:HD_A]
        n_out_ref[0] = n_sc[0:H_A, 0:HD_A]
        m_out_ref[0] = m_sc[0:H_A, :]


def _mlstm(proj, row0, nb, seq, lb, lp, convprev, c0, c0_slot, c_out_prev, layer, n0, m0,
           convw, convb, bgate, gh, out_dtype):
    n_layers = c_out_prev.shape[0]
    nc = seq // lb
    rb0 = row0 // lb
    gate_col = (4 * D_A_PAD + D_MEM) // GROUP
    scratch = [
        pltpu.VMEM((lp + 8, 2 * D_A_PAD), F32),
        pltpu.VMEM((H_A, HD_A_PAD, HD_A_PAD), F32),
        pltpu.VMEM((8, HD_A_PAD), F32),
        pltpu.VMEM((8, LANES), F32),
    ]
    if lb != lp:
        scratch += [pltpu.VMEM((lp, 4 * D_A_PAD), F32), pltpu.VMEM((lp, GROUP), F32)]
    in_specs = [
        pl.BlockSpec((lb, 4 * D_A_PAD), lambda b, c: (rb0 + b * nc + c, 0)),
        pl.BlockSpec((lb, GROUP), lambda b, c: (rb0 + b * nc + c, gate_col)),
        pl.BlockSpec((1, 8, 2 * D_A_PAD), lambda b, c: (b, 0, 0)),
        pl.BlockSpec((1, 1, H_A, HD_A, HD_A), lambda b, c: (c0_slot, b, 0, 0, 0)),
        pl.BlockSpec((1, H_A, HD_A), lambda b, c: (b, 0, 0)),
        pl.BlockSpec((1, H_A, LANES), lambda b, c: (b, 0, 0)),
        pl.BlockSpec((CONV_W, 2 * D_A_PAD), lambda b, c: (0, 0)),
        pl.BlockSpec((1, 2 * D_A_PAD), lambda b, c: (0, 0)),
        pl.BlockSpec((1, GROUP), lambda b, c: (0, 0)),
        pl.BlockSpec((1, D_A_PAD), lambda b, c: (0, 0)),
    ]
    in_specs.append(pl.BlockSpec(memory_space=pl.ANY))
    args = [proj, proj, convprev, c0, n0, m0, convw, convb, bgate, gh, c_out_prev]
    aliases = {len(args) - 1: 1}
    return pl.pallas_call(
        functools.partial(_mlstm_kernel, lb=lb, lp=lp),
        grid=(nb, nc),
        in_specs=in_specs,
        out_specs=[
            pl.BlockSpec((lb, D_A_PAD), lambda b, c: (b * nc + c, 0)),
            pl.BlockSpec((1, 1, H_A, HD_A, HD_A), lambda b, c: (layer, b, 0, 0, 0)),
            pl.BlockSpec((1, H_A, HD_A), lambda b, c: (b, 0, 0)),
            pl.BlockSpec((1, H_A, LANES), lambda b, c: (b, 0, 0)),
        ],
        out_shape=[
            jax.ShapeDtypeStruct((nb * seq, D_A_PAD), out_dtype),
            jax.ShapeDtypeStruct((n_layers, nb, H_A, HD_A, HD_A), F32),
            jax.ShapeDtypeStruct((nb, H_A, HD_A), F32),
            jax.ShapeDtypeStruct((nb, H_A, LANES), F32),
        ],
        scratch_shapes=scratch,
        input_output_aliases=aliases,
        compiler_params=_cparams(("parallel", "arbitrary")),
        name="mlstm",
    )(*args)


SB_BLOCK = 256
SB_QBLOCK = 512
LOG2E = 1.4426950408889634


def _neg_later(n, keys_on_rows):
    r = lax.broadcasted_iota(jnp.int32, (n, n), 0)
    c = lax.broadcasted_iota(jnp.int32, (n, n), 1)
    later = (c > r) if keys_on_rows else (r > c)
    return jnp.where(later, -1.0, 0.0).astype(BF16)


def _softplus2(z):
    neg_abs = lax.bitcast_convert_type(lax.bitcast_convert_type(z, jnp.uint32) | jnp.uint32(0x80000000), F32)
    return jnp.log(1.0 + jnp.exp2(neg_abs)) * LOG2E + jnp.maximum(z, 0.0)


def _sb_weights(z_streams, masks, later_mat, carries, keys_on_rows=False):
    sps = [[_softplus2(z) if m is None else jnp.where(m, _softplus2(z), 0.0) for z, m in zip(zs, masks)]
           for zs in z_streams]
    if keys_on_rows:
        inblock = [[jnp.dot(later_mat, sp.astype(BF16), preferred_element_type=F32) for sp in s] for s in sps]
    else:
        inblock = [[jnp.dot(sp.astype(BF16), later_mat, preferred_element_type=F32) for sp in s] for s in sps]
    weights, new_carries = [], []
    for zs, s, ib, carry in zip(z_streams, sps, inblock, carries):
        ws = []
        for z, sp, l, m in zip(zs, s, ib, masks):
            later = l + carry
            a = jnp.exp2(z - sp + later)
            ws.append(a if m is None else jnp.where(m, a, 0.0))
            carry = (later[0:1, :] - sp[0:1, :]) if keys_on_rows else (later[:, 0:1] - sp[:, 0:1])
        weights.append(ws)
        new_carries.append(carry)
    return weights, tuple(new_carries)


def _sb_prompt_kernel(bias_ref, q_ref, k_ref, v_ref, o_ref, kb_sc, v0_sc, v1_sc):
    p = pl.program_id(1)
    i = pl.program_id(2)
    blk = SB_BLOCK
    nh = LANES // HD_B
    v_scs = (v0_sc, v1_sc)

    @pl.when(i == 0)
    def _():
        kb_sc[...] = k_ref[...].astype(BF16)
        v = v_ref[...]
        vlane = lax.broadcasted_iota(jnp.int32, v.shape, 1) // HD_B
        for hh in range(nh):
            v_scs[hh][...] = jnp.where(vlane == hh, v, 0.0).astype(BF16)

    q = q_ref[...]
    tq = q.shape[0]
    per_q = tq // blk
    lane = lax.broadcasted_iota(jnp.int32, q.shape, 1) // HD_B
    later_mat = _neg_later(blk, keys_on_rows=False)
    qh = [(jnp.where(lane == hh, q, 0.0) * (HD_B ** -0.5 * LOG2E)).astype(BF16) for hh in range(nh)]
    bias = [bias_ref[p * nh + hh] * LOG2E for hh in range(nh)]

    def tiles(js, masks, carries, acc):
        offs = [pl.multiple_of(j * blk, blk) for j in js]
        kbs = [kb_sc[pl.ds(off, blk), :] for off in offs]
        zs = [[lax.dot_general(qh[hh], kb, (((1,), (1,)), ((), ())), preferred_element_type=F32) + bias[hh]
               for kb in kbs] for hh in range(nh)]
        ws, carries = _sb_weights(zs, masks, later_mat, carries)
        for hh in range(nh):
            for off, a in zip(offs, ws[hh]):
                acc = acc + jnp.dot(a.astype(BF16), v_scs[hh][pl.ds(off, blk), :], preferred_element_type=F32)
        return carries, acc

    rel = lax.broadcasted_iota(jnp.int32, (tq, blk), 0) - lax.broadcasted_iota(jnp.int32, (tq, blk), 1)
    first = i * per_q
    diag_js = [first + t for t in reversed(range(per_q))]
    diag_masks = [rel > t * blk for t in reversed(range(per_q))]
    zero_c = tuple(jnp.zeros((tq, 1), F32) for _ in range(nh))
    carries, acc = tiles(diag_js, diag_masks, zero_c, jnp.zeros(q.shape, F32))

    def body(t, state):
        j = first - 1 - t * per_q
        return tiles([j - u for u in range(per_q)], [None] * per_q, state[0], state[1])

    _, acc = lax.fori_loop(0, i, body, (carries, acc))
    o_ref[...] = acc.astype(o_ref.dtype)


def _sb_prompt(qarr, kvarr, bias, nb, seq):
    blk = SB_QBLOCK
    npair = D_MAIN // LANES
    ni = seq // blk
    grid_spec = pltpu.PrefetchScalarGridSpec(
        num_scalar_prefetch=1,
        grid=(nb, npair, ni),
        in_specs=[
            pl.BlockSpec((blk, LANES), lambda b, p, i, s: (b * ni + i, p)),
            pl.BlockSpec((seq, LANES), lambda b, p, i, s: (b, p)),
            pl.BlockSpec((seq, LANES), lambda b, p, i, s: (b, npair + p)),
        ],
        out_specs=pl.BlockSpec((blk, LANES), lambda b, p, i, s: (b * ni + i, p)),
        scratch_shapes=[pltpu.VMEM((seq, LANES), BF16)] * (1 + LANES // HD_B),
    )
    return pl.pallas_call(
        _sb_prompt_kernel,
        grid_spec=grid_spec,
        out_shape=jax.ShapeDtypeStruct((nb * seq, D_MAIN), BF16),
        compiler_params=_cparams(("parallel", "parallel", "arbitrary")),
        name="sb_prompt",
    )(bias, qarr, kvarr, kvarr)


def _sb_sample_kernel(pt_ref, q_ref, knew_ref, vnew_ref, *refs, lq, page, npage):
    kt_refs, vt_refs = refs[:npage], refs[npage:2 * npage]
    bias_ref, o_ref, pad_sc = refs[2 * npage:]
    rows = H_B * lq
    later_mat = _neg_later(page, keys_on_rows=True)
    tn_dims = (((0,), (0,)), ((), ()))

    q = q_ref[...]
    lane = lax.broadcasted_iota(jnp.int32, q.shape, 1) // HD_B
    qx = jnp.concatenate([jnp.where(lane == h, q, 0.0) for h in range(H_B)]
                         + [jnp.zeros((page - rows, D_MAIN), F32)], axis=0)
    qxt = (qx * (HD_B ** -0.5 * LOG2E)).T.astype(BF16)
    bias = bias_ref[...] * LOG2E

    pad_sc[...] = jnp.zeros_like(pad_sc)
    pad_sc[0:lq, :] = knew_ref[...]
    knew = pad_sc[...].astype(BF16)
    pad_sc[0:lq, :] = vnew_ref[...]
    vnew = pad_sc[...].astype(BF16)
    kidx = lax.broadcasted_iota(jnp.int32, (page, page), 0)
    qidx = lax.broadcasted_iota(jnp.int32, (page, page), 1) % lq
    order = list(reversed(range(npage)))

    zs = [jnp.dot(knew, qxt, preferred_element_type=F32) + bias]
    zs += [lax.dot_general(kt_refs[g][0].astype(BF16), qxt, tn_dims, preferred_element_type=F32) + bias
           for g in order]
    masks = [kidx < qidx] + [None] * npage
    (ws,), _ = _sb_weights([zs], masks, later_mat, (jnp.zeros((1, page), F32),), keys_on_rows=True)

    acc = lax.dot_general(vnew, ws[0].astype(BF16), tn_dims, preferred_element_type=F32)
    for g, a in zip(order, ws[1:]):
        acc = acc + jnp.dot(vt_refs[g][0].astype(BF16), a.astype(BF16), preferred_element_type=F32)

    acc_t = acc.T
    olane = lax.broadcasted_iota(jnp.int32, (lq, D_MAIN), 1) // HD_B
    out = jnp.zeros((lq, D_MAIN), F32)
    for h in range(H_B):
        out = out + jnp.where(olane == h, acc_t[h * lq:(h + 1) * lq, :], 0.0)
    o_ref[...] = out


def _sb_sample(qarr, kvarr, row0, nb, lq, cache_kt, cache_vt, page_table, bias_cols):
    npage = page_table.shape[1]
    page = cache_kt.shape[2]
    rb0 = row0 // lq
    page_specs = [pl.BlockSpec((1, D_MAIN, page), lambda b, pt, g=g: (pt[b, g], 0, 0)) for g in range(npage)]
    grid_spec = pltpu.PrefetchScalarGridSpec(
        num_scalar_prefetch=1,
        grid=(nb,),
        in_specs=[
            pl.BlockSpec((lq, D_MAIN), lambda b, pt: (rb0 + b, 0)),
            pl.BlockSpec((lq, D_MAIN), lambda b, pt: (rb0 + b, 0)),
            pl.BlockSpec((lq, D_MAIN), lambda b, pt: (rb0 + b, 1)),
        ] + page_specs + page_specs + [pl.BlockSpec((1, page), lambda b, pt: (0, 0))],
        out_specs=pl.BlockSpec((lq, D_MAIN), lambda b, pt: (b, 0)),
        scratch_shapes=[pltpu.VMEM((page, D_MAIN), F32)],
    )
    return pl.pallas_call(
        functools.partial(_sb_sample_kernel, lq=lq, page=page, npage=npage),
        grid_spec=grid_spec,
        out_shape=jax.ShapeDtypeStruct((nb * lq, D_MAIN), F32),
        compiler_params=_cparams(("parallel",)),
        name="sb_sample",
    )(page_table, qarr, kvarr, kvarr, *([cache_kt] * npage), *([cache_vt] * npage), bias_cols)


def _pad_heads(w):
    lead = w.shape[:-1]
    w = w.reshape(lead + (H_A, HD_A))
    w = jnp.pad(w, [(0, 0)] * len(lead) + [(0, 0), (0, HD_A_PAD - HD_A)])
    return w.reshape(lead + (D_A_PAD,))


def _unpad_heads(w, nh):
    lead = w.shape[:-1]
    return w.reshape(lead + (nh, HD_A_PAD))[..., :HD_A].reshape(lead + (nh * HD_A,))


def _ffn_weights(w_up, w_down):
    nchunk = D_FF // FF_CHUNK
    wg = w_up[:, :D_FF].reshape(D_MODEL, nchunk, FF_CHUNK)
    wu = w_up[:, D_FF:].reshape(D_MODEL, nchunk, FF_CHUNK)
    wup_r = jnp.concatenate([wg, wu], axis=-1).transpose(1, 0, 2).astype(BF16)
    return wup_r, w_down.astype(BF16)


def kernel(x_prompt, x_sample, cache_k, cache_v, cache_mem_k, cache_mem_v, state_C, state_n, state_m, state_conv, page_table, mem_prompt, g_ffn1, w_ffn1_up, w_ffn1_down, g_mix, w_in_a, b_gate_a, conv_w_a, conv_b_a, g_h_a, g_kv, w_kv, g_k_b, w_in_b, g_q_b, b_sb, g_mem, w_mem_kv, g_q_mem, g_k_mem, w_out, g_ffn2, w_ffn2_up, w_ffn2_down):
    bp, seq, d = x_prompt.shape
    bs, lq, _ = x_sample.shape
    depth = g_ffn1.shape[0]
    n_a = w_in_a.shape[0]
    tp, ts = bp * seq, bs * lq
    mem_len = mem_prompt.shape[1]
    n_phys, page = cache_k.shape[0], cache_k.shape[1]

    x = jnp.concatenate([x_prompt.reshape(tp, d), x_sample.reshape(bs * lq, d)], axis=0)
    ones_d = jnp.ones((d,), F32)

    mem_rows = mem_prompt.reshape(bp * mem_len, d)
    mem_kv_p = []
    for l in range(depth):
        gain = jnp.concatenate([jnp.tile(g_k_mem[l], H_MEM), jnp.ones((D_MEM,), F32)])
        mem_kv_p.append(_proj(mem_rows, g_mem[l], w_mem_kv[l].astype(BF16), gain, (0,), tn=2 * D_MEM,
                              tm=bp * mem_len, head_width=HD_MEM))
    cmkt = cache_mem_k.transpose(0, 1, 3, 4, 2).reshape(depth * bs * D_MEM, mem_len)
    cmvt = cache_mem_v.transpose(0, 1, 3, 4, 2).reshape(depth * bs * D_MEM, mem_len)
    ckt = cache_k.transpose(0, 2, 3, 1).reshape(n_phys, D_MAIN, page)
    cvt = cache_v.transpose(0, 2, 3, 1).reshape(n_phys, D_MAIN, page)

    convs_p, convs_s, ns_p, ns_s, ms_p, ms_s = [], [], [], [], [], []
    c_p = jnp.zeros((n_a, bp, H_A, HD_A, HD_A), F32)
    c_s = jnp.zeros((n_a, bs, H_A, HD_A, HD_A), F32)
    c0_p = jnp.zeros((1, bp, H_A, HD_A, HD_A), F32)
    kv = None
    for l in range(depth):
        if l == n_a:
            gain = jnp.concatenate([jnp.tile(g_k_b, H_B), jnp.ones((D_MAIN,), F32)])
            kv = _proj(x, g_kv, w_kv.astype(BF16), gain, tuple(range(D_MAIN // GROUP)), tn=D_MAIN)
        x = _ffn(x, g_ffn1[l], *_ffn_weights(w_ffn1_up[l], w_ffn1_down[l]))
        wo = w_out[l]
        if l < n_a:
            wi = w_in_a[l]
            q_w, k_w = wi[:, :D_MAIN], wi[:, D_MAIN:2 * D_MAIN]
            v_w, o_w = wi[:, 2 * D_MAIN:3 * D_MAIN], wi[:, 3 * D_MAIN:4 * D_MAIN]
            gate_w = wi[:, 4 * D_MAIN:4 * D_MAIN + 2 * H_A]
            qm_w = wi[:, 4 * D_MAIN + 2 * H_A:]
            w_a = jnp.concatenate([_pad_heads(q_w), _pad_heads(k_w), _pad_heads(v_w), _pad_heads(o_w), qm_w,
                                   jnp.pad(gate_w, ((0, 0), (0, GROUP - 2 * H_A)))], axis=1).astype(BF16)
            n_cols = w_a.shape[1]
            gain = jnp.concatenate([jnp.ones((4 * D_A_PAD,), F32), jnp.tile(g_q_mem[l], H_MEM),
                                    jnp.ones((GROUP,), F32)])
            proj = _proj(x, g_mix[l], w_a, gain, (4 * D_A_PAD // GROUP,), tn=2 * GROUP, head_width=HD_MEM)
            qm_col = 4 * D_A_PAD // D_MEM

            convw = jnp.concatenate([_pad_heads(conv_w_a[l][:, :D_MAIN]), _pad_heads(conv_w_a[l][:, D_MAIN:])], axis=1)
            convb = jnp.concatenate([_pad_heads(conv_b_a[l][:D_MAIN]), _pad_heads(conv_b_a[l][D_MAIN:])]).reshape(1, -1)
            bgate = jnp.pad(b_gate_a[l], (0, GROUP - 2 * H_A)).reshape(1, GROUP)
            gh = _pad_heads(g_h_a[l].reshape(-1)).reshape(1, D_A_PAD)

            def conv_state(c):
                c = jnp.concatenate([_pad_heads(c[..., :D_MAIN]), _pad_heads(c[..., D_MAIN:])], axis=-1)
                return jnp.pad(c, ((0, 0), (8 - (CONV_W - 1), 0), (0, 0)))

            main_p, c_p, n_p, m_p = _mlstm(
                proj, 0, bp, seq, SB_BLOCK, SB_BLOCK,
                conv_state(jnp.zeros((bp, CONV_W - 1, 2 * D_MAIN), F32)), c0_p, 0, c_p, l,
                jnp.zeros((bp, H_A, HD_A), F32), jnp.zeros((bp, H_A, LANES), F32),
                convw, convb, bgate, gh, BF16)
            main_s, c_s, n_s, m_s = _mlstm(
                proj, tp, bs, lq, lq, LANES, conv_state(state_conv[l]), state_C, l, c_s, l, state_n[l],
                jnp.broadcast_to(state_m[l][..., None], (bs, H_A, LANES)), convw, convb, bgate, gh, F32)
            w_main = _pad_heads(wo[:D_MAIN].T).T.astype(BF16)

            tail = CONV_W - 1
            conv_rows_p = jnp.stack([proj[(b + 1) * seq - tail:(b + 1) * seq, :2 * D_A_PAD] for b in range(bp)])
            conv_rows_s = proj[tp:].reshape(bs, lq, -1)[:, lq - tail:, :2 * D_A_PAD]
            convs_p.append(_unpad_heads(conv_rows_p, 2 * H_A))
            convs_s.append(_unpad_heads(conv_rows_s, 2 * H_A))
            ns_p.append(n_p); ms_p.append(m_p[..., 0])
            ns_s.append(n_s); ms_s.append(m_s[..., 0])
        else:
            j = l - n_a
            gain = jnp.concatenate([jnp.tile(g_q_b[j], H_B), jnp.tile(g_q_mem[l], H_MEM)])
            proj = _proj(x, g_mix[l], w_in_b[j].astype(BF16), gain, tuple(range(D_MODEL // GROUP)), tn=2 * GROUP)
            qm_col = D_MAIN // D_MEM
            main_p = _sb_prompt(proj, kv, b_sb[j], bp, seq)
            bias_cols = jnp.pad(jnp.repeat(b_sb[j], lq), (0, page - H_B * lq)).reshape(1, page)
            main_s = _sb_sample(proj, kv, tp, bs, lq, ckt, cvt, page_table, bias_cols)
            w_main = wo[:D_MAIN].astype(BF16)

        mem_p = _memattn(proj, qm_col, 0, bp, seq, 512, mem_kv_p[l], 0, mem_kv_p[l], 1, BF16)
        mem_s = _memattn_small(proj, qm_col, tp, bs, lq, mem_len, cmkt, cmvt, l * bs)
        w_mem = wo[D_MAIN:].astype(BF16)
        x = _outproj(x, 0, main_p, w_main, mem_p, w_mem)
        x = _outproj(x, tp, main_s, w_main, mem_s, w_mem)
        ffn2 = _ffn_weights(w_ffn2_up[l], w_ffn2_down[l])
        if l < depth - 1:
            x = _ffn(x, g_ffn2[l], *ffn2)
        else:
            y_p = _ffn(x, g_ffn2[l], *ffn2, row0=0, nrows=tp)
            y_s = _ffn(x, g_ffn2[l], *ffn2, row0=tp, nrows=ts)

    mem_k_prompt = jnp.stack([m[:, :D_MEM] for m in mem_kv_p]).reshape(depth, bp, mem_len, H_MEM, HD_MEM)
    mem_v_prompt = jnp.stack([m[:, D_MEM:] for m in mem_kv_p]).reshape(depth, bp, mem_len, H_MEM, HD_MEM)
    return (y_p.reshape(bp, seq, d), y_s.reshape(bs, lq, d),
            kv[:tp, :D_MAIN].reshape(bp, seq, H_B, HD_B), kv[:tp, D_MAIN:].reshape(bp, seq, H_B, HD_B),
            kv[tp:, :D_MAIN].reshape(bs, lq, H_B, HD_B), kv[tp:, D_MAIN:].reshape(bs, lq, H_B, HD_B),
            mem_k_prompt, mem_v_prompt,
            jnp.stack(convs_p), c_p, jnp.stack(ns_p), jnp.stack(ms_p),
            jnp.stack(convs_s), c_s, jnp.stack(ns_s), jnp.stack(ms_s))
```

```python
import functools

import jax
import jax.numpy as jnp
from jax import lax
from jax.experimental import pallas as pl
from jax.experimental.pallas import tpu as pltpu

F32 = jnp.float32
BF16 = jnp.bfloat16

EPS = 1e-6
D_MODEL = 1024
H_MEM, HD_MEM = 4, 64
D_MEM = H_MEM * HD_MEM
D_MAIN = D_MODEL - D_MEM
H_A, HD_A = 4, 192
H_B, HD_B = 12, 64
HD_A_PAD = 256
D_A_PAD = H_A * HD_A_PAD
CONV_W = 4
D_FF = 2816
FF_CHUNK = 256
LANES = 128
GROUP = 256
ROW_TILE = 1024
FFN_ROW_TILE = 1024
VMEM_LIMIT = 56 * 1024 * 1024


def _cparams(sem, vmem=VMEM_LIMIT):
    return pltpu.CompilerParams(dimension_semantics=sem, vmem_limit_bytes=vmem)


def _rms(x, g):
    return x * lax.rsqrt(jnp.mean(x * x, axis=-1, keepdims=True) + EPS) * g


def _sigmoid(x):
    return 1.0 / (1.0 + jnp.exp(-x))


def _softplus(x):
    return jnp.maximum(x, 0.0) + jnp.log(1.0 + jnp.exp(-jnp.abs(x)))


def _head_group_matrix(n, width):
    r = lax.broadcasted_iota(jnp.int32, (n, n), 0) // width
    c = lax.broadcasted_iota(jnp.int32, (n, n), 1) // width
    return (r == c).astype(BF16)


def _ffn_kernel(x_ref, g_ref, wup_ref, wdn_ref, o_ref, h_sc, acc_sc):
    h_sc[...] = _rms(x_ref[...], g_ref[...]).astype(BF16)
    for c in range(wup_ref.shape[0]):
        gu = jnp.dot(h_sc[...], wup_ref[c], preferred_element_type=F32)
        g = gu[:, :FF_CHUNK]
        u = gu[:, FF_CHUNK:]
        act = (g * _sigmoid(g) * u).astype(BF16)
        down = jnp.dot(act, wdn_ref[c * FF_CHUNK:(c + 1) * FF_CHUNK, :], preferred_element_type=F32)
        if c == 0:
            acc_sc[...] = down
        else:
            acc_sc[...] += down
    o_ref[...] = x_ref[...] + 0.5 * acc_sc[...]


def _ffn(x, g, wup_r, wdn, row0=0, nrows=None):
    t, d = x.shape
    nrows = t if nrows is None else nrows
    nchunk = wup_r.shape[0]
    tm = FFN_ROW_TILE
    rb0 = row0 // tm
    resident = pl.Buffered(1)
    return pl.pallas_call(
        _ffn_kernel,
        grid=(nrows // tm,),
        in_specs=[
            pl.BlockSpec((tm, d), lambda i: (rb0 + i, 0)),
            pl.BlockSpec((1, d), lambda i: (0, 0)),
            pl.BlockSpec((nchunk, d, 2 * FF_CHUNK), lambda i: (0, 0, 0), pipeline_mode=resident),
            pl.BlockSpec((nchunk * FF_CHUNK, d), lambda i: (0, 0), pipeline_mode=resident),
        ],
        out_specs=pl.BlockSpec((tm, d), lambda i: (i, 0)),
        out_shape=jax.ShapeDtypeStruct((nrows, d), F32),
        scratch_shapes=[pltpu.VMEM((tm, d), BF16), pltpu.VMEM((tm, d), F32)],
        compiler_params=_cparams(("parallel",)),
        name="ffn",
    )(x, g.reshape(1, d), wup_r, wdn)


def _proj_kernel(x_ref, g_ref, w_ref, gain_ref, o_ref, h_sc, *, normed, head_width):
    j = pl.program_id(1)

    @pl.when(j == 0)
    def _():
        h_sc[...] = _rms(x_ref[...], g_ref[...]).astype(BF16)

    o_ref[...] = jnp.dot(h_sc[...], w_ref[...], preferred_element_type=F32)

    for tile, k in normed:
        @pl.when(j == tile)
        def _(k=k):
            sl = slice(k * GROUP, (k + 1) * GROUP)
            y = o_ref[:, sl]
            ss = jnp.dot((y * y).astype(BF16), _head_group_matrix(GROUP, head_width),
                         preferred_element_type=F32)
            o_ref[:, sl] = y * lax.rsqrt(ss * (1.0 / head_width) + EPS) * gain_ref[:, sl]


def _proj(x, g, w, gain, normed_groups, tn, tm=ROW_TILE, head_width=HD_B):
    t, d = x.shape
    n = w.shape[1]
    nsub = tn // GROUP
    normed = tuple((gi // nsub, gi % nsub) for gi in normed_groups)
    return pl.pallas_call(
        functools.partial(_proj_kernel, normed=normed, head_width=head_width),
        grid=(t // tm, n // tn),
        in_specs=[
            pl.BlockSpec((tm, d), lambda i, j: (i, 0)),
            pl.BlockSpec((1, d), lambda i, j: (0, 0)),
            pl.BlockSpec((d, tn), lambda i, j: (0, j)),
            pl.BlockSpec((1, tn), lambda i, j: (0, j)),
        ],
        out_specs=pl.BlockSpec((tm, tn), lambda i, j: (i, j)),
        out_shape=jax.ShapeDtypeStruct((t, n), F32),
        scratch_shapes=[pltpu.VMEM((tm, d), BF16)],
        compiler_params=_cparams(("parallel", "arbitrary")),
        name="proj",
    )(x, g.reshape(1, d), w, gain.reshape(1, n))


def _outproj_kernel(x_ref, a_ref, wa_ref, b_ref, wb_ref, o_ref):
    o_ref[...] = (x_ref[...]
                  + jnp.dot(a_ref[...].astype(BF16), wa_ref[...], preferred_element_type=F32)
                  + jnp.dot(b_ref[...].astype(BF16), wb_ref[...], preferred_element_type=F32))


def _outproj(x, row0, a, wa, b, wb):
    t, d = x.shape
    tm = ROW_TILE
    nrows, ka = a.shape
    kb = b.shape[1]
    rb0 = row0 // tm
    return pl.pallas_call(
        _outproj_kernel,
        grid=(nrows // tm,),
        in_specs=[
            pl.BlockSpec((tm, d), lambda i: (rb0 + i, 0)),
            pl.BlockSpec((tm, ka), lambda i: (i, 0)),
            pl.BlockSpec((ka, d), lambda i: (0, 0)),
            pl.BlockSpec((tm, kb), lambda i: (i, 0)),
            pl.BlockSpec((kb, d), lambda i: (0, 0)),
        ],
        out_specs=pl.BlockSpec((tm, d), lambda i: (rb0 + i, 0)),
        out_shape=jax.ShapeDtypeStruct((t, d), F32),
        input_output_aliases={0: 0},
        compiler_params=_cparams(("parallel",)),
        name="outproj",
    )(x, a, wa, b, wb)


def _memattn_kernel(q_ref, k_ref, v_ref, o_ref):
    q = q_ref[...]
    kb = k_ref[...].astype(BF16)
    vb = v_ref[...].astype(BF16)
    lane = lax.broadcasted_iota(jnp.int32, q.shape, 1) // HD_MEM
    acc = jnp.zeros(q.shape, F32)
    for h in range(H_MEM):
        qh = jnp.where(lane == h, q, 0.0).astype(BF16)
        s = lax.dot_general(qh, kb, (((1,), (1,)), ((), ())), preferred_element_type=F32) * (HD_MEM ** -0.5)
        e = jnp.exp(s - jnp.max(s, axis=-1, keepdims=True))
        p = (e / jnp.sum(e, axis=-1, keepdims=True)).astype(BF16)
        acc = acc + jnp.where(lane == h, jnp.dot(p, vb, preferred_element_type=F32), 0.0)
    o_ref[...] = acc.astype(o_ref.dtype)


def _memattn(qarr, qcol, row0, nb, lq, tq, karr, kcol, varr, vcol, out_dtype):
    m = karr.shape[0] // nb
    nq = lq // tq
    rb0 = row0 // tq
    return pl.pallas_call(
        _memattn_kernel,
        grid=(nb, nq),
        in_specs=[
            pl.BlockSpec((tq, D_MEM), lambda b, i: (rb0 + b * nq + i, qcol)),
            pl.BlockSpec((m, D_MEM), lambda b, i: (b, kcol)),
            pl.BlockSpec((m, D_MEM), lambda b, i: (b, vcol)),
        ],
        out_specs=pl.BlockSpec((tq, D_MEM), lambda b, i: (b * nq + i, 0)),
        out_shape=jax.ShapeDtypeStruct((nb * lq, D_MEM), out_dtype),
        compiler_params=_cparams(("parallel", "parallel")),
        name="memattn",
    )(qarr, karr, varr)


def _memattn_small_kernel(q_ref, k_ref, v_ref, o_ref, *, nseq, lq):
    lane = lax.broadcasted_iota(jnp.int32, (lq, D_MEM), 1) // HD_MEM
    scores = []
    for s in range(nseq):
        q = q_ref[s * lq:(s + 1) * lq, :]
        qx = jnp.concatenate([jnp.where(lane == h, q, 0.0) for h in range(H_MEM)], axis=0).astype(BF16)
        kt = k_ref[s * D_MEM:(s + 1) * D_MEM, :].astype(BF16)
        scores.append(jnp.dot(qx, kt, preferred_element_type=F32) * (HD_MEM ** -0.5))
    probs = []
    for sc in scores:
        e = jnp.exp(sc - jnp.max(sc, axis=-1, keepdims=True))
        probs.append((e / jnp.sum(e, axis=-1, keepdims=True)).astype(BF16))
    for s, p in enumerate(probs):
        vt = v_ref[s * D_MEM:(s + 1) * D_MEM, :].astype(BF16)
        ox = lax.dot_general(p, vt, (((1,), (1,)), ((), ())), preferred_element_type=F32)
        out = jnp.zeros((lq, D_MEM), F32)
        for h in range(H_MEM):
            out = out + jnp.where(lane == h, ox[h * lq:(h + 1) * lq, :], 0.0)
        o_ref[s * lq:(s + 1) * lq, :] = out


def _memattn_small(qarr, qcol, row0, nb, lq, m, ktarr, vtarr, seq0, nseq=8):
    rb0 = row0 // (nseq * lq)
    kb0 = seq0 // nseq
    return pl.pallas_call(
        functools.partial(_memattn_small_kernel, nseq=nseq, lq=lq),
        grid=(nb // nseq,),
        in_specs=[
            pl.BlockSpec((nseq * lq, D_MEM), lambda i: (rb0 + i, qcol)),
            pl.BlockSpec((nseq * D_MEM, m), lambda i: (kb0 + i, 0)),
            pl.BlockSpec((nseq * D_MEM, m), lambda i: (kb0 + i, 0)),
        ],
        out_specs=pl.BlockSpec((nseq * lq, D_MEM), lambda i: (i, 0)),
        out_shape=jax.ShapeDtypeStruct((nb * lq, D_MEM), F32),
        compiler_params=_cparams(("parallel",)),
        name="memattn_small",
    )(qarr, ktarr, vtarr)


def _mlstm_kernel(qkvo_ref, gates_ref, convprev_ref, c0_ref, n0_ref, m0_ref,
                  convw_ref, convb_ref, bgate_ref, gh_ref, c_all_ref,
                  h_ref, c_out_ref, n_out_ref, m_out_ref,
                  ext_sc, c_sc, n_sc, m_sc, *stage, lb, lp):
    ci = pl.program_id(1)
    last = lb - 1

    @pl.when(ci == 0)
    def _():
        ext_sc[0:8, :] = convprev_ref[0]
        c_sc[...] = jnp.zeros_like(c_sc)
        n_sc[...] = jnp.zeros_like(n_sc)
        for h in range(H_A):
            c_sc[h, 0:HD_A, 0:HD_A] = c0_ref[0, 0, h]
        n_sc[0:H_A, 0:HD_A] = n0_ref[0]
        m_sc[0:H_A, :] = m0_ref[0]

    if lb == lp:
        src, gsrc = qkvo_ref, gates_ref
    else:
        src, gsrc = stage
        src[...] = jnp.zeros_like(src)
        gsrc[...] = jnp.zeros_like(gsrc)
        src[0:lb, :] = qkvo_ref[...]
        gsrc[0:lb, :] = gates_ref[...]

    ext_sc[8:8 + lp, :] = src[:, 0:2 * D_A_PAD]

    row = lax.broadcasted_iota(jnp.int32, (lp, lp), 0)
    col = lax.broadcasted_iota(jnp.int32, (lp, lp), 1)
    causal = row >= col
    valid_row = lax.broadcasted_iota(jnp.int32, (lp, 1), 0) < lb

    gl = lax.broadcasted_iota(jnp.int32, (lp, GROUP), 1)
    gts = gsrc[...] + bgate_ref[...]
    gts = jnp.where((gl >= H_A) & (gl < 2 * H_A), -_softplus(-gts), gts)
    gts = jnp.where(valid_row & (gl < 2 * H_A), gts, 0.0)
    cum = jnp.dot(causal.astype(F32), gts, preferred_element_type=F32, precision=lax.Precision.HIGHEST)
    gts_t = gts.T
    cum_t = cum.T

    def conv_silu(c0):
        sl = slice(c0, c0 + HD_A_PAD)
        a = convb_ref[:, sl]
        for j in range(CONV_W):
            a = a + ext_sc[pl.ds(8 - (CONV_W - 1) + j, lp), sl] * convw_ref[j:j + 1, sl]
        a = a * _sigmoid(a)
        return a if lb == lp else jnp.where(valid_row, a, 0.0)

    heads = []
    for h in range(H_A):
        q = conv_silu(h * HD_A_PAD)
        k = conv_silu(D_A_PAD + h * HD_A_PAD) * (HD_A ** -0.5)
        vb = src[:, 2 * D_A_PAD + h * HD_A_PAD:2 * D_A_PAD + (h + 1) * HD_A_PAD].astype(BF16)
        qb, kb = q.astype(BF16), k.astype(BF16)
        s = lax.dot_general(qb, kb, (((1,), (1,)), ((), ())), preferred_element_type=F32)
        cq = lax.dot_general(qb, c_sc[h].astype(BF16), (((1,), (1,)), ((), ())), preferred_element_type=F32)
        qn = jnp.sum(q * n_sc[h:h + 1, :], axis=-1, keepdims=True)
        heads.append((k, vb, s, cq, qn))

    for h, (k, vb, s, cq, qn) in enumerate(heads):
        hs = slice(h * HD_A_PAD, (h + 1) * HD_A_PAD)
        o_pre = src[:, 3 * D_A_PAD + h * HD_A_PAD:3 * D_A_PAD + (h + 1) * HD_A_PAD]

        ig_col = gts[:, h:h + 1]
        b_col = cum[:, H_A + h:H_A + h + 1]
        ig_row = gts_t[h:h + 1, :]
        b_row = cum_t[H_A + h:H_A + h + 1, :]
        m_prev = m_sc[h:h + 1, 0:1]
        c_prev = c_sc[h]
        n_prev = n_sc[h:h + 1, :]

        dlog = jnp.where(causal, b_col - b_row + ig_row, -jnp.inf)
        inter = b_col + m_prev
        m_t = jnp.maximum(inter, jnp.max(dlog, axis=-1, keepdims=True))
        w = jnp.exp(dlog - m_t) * s
        w_inter = jnp.exp(inter - m_t)
        num = jnp.dot(w.astype(BF16), vb, preferred_element_type=F32) + cq * w_inter
        den = jnp.sum(w, axis=-1, keepdims=True) + w_inter * qn
        den = jnp.maximum(jnp.abs(den), jnp.exp(-m_t))
        hh = num / den
        hn = hh * lax.rsqrt(jnp.sum(hh * hh, axis=-1, keepdims=True) * (1.0 / HD_A) + EPS) * gh_ref[:, hs]
        out = hn * _sigmoid(o_pre)
        h_ref[:, hs] = out[0:lb, :].astype(h_ref.dtype)

        m_new = m_t[last:last + 1, :]
        b_last = b_col[last:last + 1, :]
        w_end = jnp.exp(b_last - b_col + ig_col - m_new)
        decay = jnp.exp(b_last + m_prev - m_new)
        kw = k * w_end
        c_sc[h] = decay * c_prev + lax.dot_general(vb, kw.astype(BF16), (((0,), (0,)), ((), ())),
                                                   preferred_element_type=F32)
        n_sc[h:h + 1, :] = decay * n_prev + jnp.sum(kw, axis=0, keepdims=True)
        m_sc[h:h + 1, :] = jnp.broadcast_to(m_new, (1, LANES))

    ext_sc[0:8, :] = ext_sc[lp:lp + 8, :]

    @pl.when(ci == pl.num_programs(1) - 1)
    def _():
        for h in range(H_A):
            c_out_ref[0, 0, h] = c_sc[h, 0:HD_A, 0:HD_A]
        n_out_ref[0] = n_sc[0:H_A, 0:HD_A]
        m_out_ref[0] = m_sc[0:H_A, :]


def _mlstm(proj, row0, nb, seq, lb, lp, convprev, c0, c0_slot, c_out_prev, layer, n0, m0,
           convw, convb, bgate, gh, out_dtype):
    n_layers = c_out_prev.shape[0]
    nc = seq // lb
    rb0 = row0 // lb
    gate_col = (4 * D_A_PAD + D_MEM) // GROUP
    scratch = [
        pltpu.VMEM((lp + 8, 2 * D_A_PAD), F32),
        pltpu.VMEM((H_A, HD_A_PAD, HD_A_PAD), F32),
        pltpu.VMEM((8, HD_A_PAD), F32),
        pltpu.VMEM((8, LANES), F32),
    ]
    if lb != lp:
        scratch += [pltpu.VMEM((lp, 4 * D_A_PAD), F32), pltpu.VMEM((lp, GROUP), F32)]
    in_specs = [
        pl.BlockSpec((lb, 4 * D_A_PAD), lambda b, c: (rb0 + b * nc + c, 0)),
        pl.BlockSpec((lb, GROUP), lambda b, c: (rb0 + b * nc + c, gate_col)),
        pl.BlockSpec((1, 8, 2 * D_A_PAD), lambda b, c: (b, 0, 0)),
        pl.BlockSpec((1, 1, H_A, HD_A, HD_A), lambda b, c: (c0_slot, b, 0, 0, 0)),
        pl.BlockSpec((1, H_A, HD_A), lambda b, c: (b, 0, 0)),
        pl.BlockSpec((1, H_A, LANES), lambda b, c: (b, 0, 0)),
        pl.BlockSpec((CONV_W, 2 * D_A_PAD), lambda b, c: (0, 0)),
        pl.BlockSpec((1, 2 * D_A_PAD), lambda b, c: (0, 0)),
        pl.BlockSpec((1, GROUP), lambda b, c: (0, 0)),
        pl.BlockSpec((1, D_A_PAD), lambda b, c: (0, 0)),
    ]
    in_specs.append(pl.BlockSpec(memory_space=pl.ANY))
    args = [proj, proj, convprev, c0, n0, m0, convw, convb, bgate, gh, c_out_prev]
    aliases = {len(args) - 1: 1}
    return pl.pallas_call(
        functools.partial(_mlstm_kernel, lb=lb, lp=lp),
        grid=(nb, nc),
        in_specs=in_specs,
        out_specs=[
            pl.BlockSpec((lb, D_A_PAD), lambda b, c: (b * nc + c, 0)),
            pl.BlockSpec((1, 1, H_A, HD_A, HD_A), lambda b, c: (layer, b, 0, 0, 0)),
            pl.BlockSpec((1, H_A, HD_A), lambda b, c: (b, 0, 0)),
            pl.BlockSpec((1, H_A, LANES), lambda b, c: (b, 0, 0)),
        ],
        out_shape=[
            jax.ShapeDtypeStruct((nb * seq, D_A_PAD), out_dtype),
            jax.ShapeDtypeStruct((n_layers, nb, H_A, HD_A, HD_A), F32),
            jax.ShapeDtypeStruct((nb, H_A, HD_A), F32),
            jax.ShapeDtypeStruct((nb, H_A, LANES), F32),
        ],
        scratch_shapes=scratch,
        input_output_aliases=aliases,
        compiler_params=_cparams(("parallel", "arbitrary")),
        name="mlstm",
    )(*args)


SB_BLOCK = 256
SB_QBLOCK = 1024
LOG2E = 1.4426950408889634


def _neg_later(n, keys_on_rows):
    r = lax.broadcasted_iota(jnp.int32, (n, n), 0)
    c = lax.broadcasted_iota(jnp.int32, (n, n), 1)
    later = (c > r) if keys_on_rows else (r > c)
    return jnp.where(later, -1.0, 0.0).astype(BF16)


def _softplus2(z):
    return jnp.log(1.0 + jnp.exp2(-jnp.abs(z))) * LOG2E + jnp.maximum(z, 0.0)


def _sb_weights(z_streams, masks, later_mat, carries, keys_on_rows=False, row0s=None):
    first = (lambda x: x[0:1, :]) if keys_on_rows else (lambda x: x[:, 0:1])
    row0s = [0] * len(masks) if row0s is None else row0s
    own, sp_first, inblock = [], [], []
    for zs in z_streams:
        o_s, f_s, i_s = [], [], []
        for z, m in zip(zs, masks):
            sp = _softplus2(z) if m is None else jnp.where(m, _softplus2(z), 0.0)
            o_s.append(z - sp)
            f_s.append(first(sp))
            spb = sp.astype(BF16)
            i_s.append(jnp.dot(later_mat, spb, preferred_element_type=F32) if keys_on_rows
                       else jnp.dot(spb, later_mat, preferred_element_type=F32))
        own.append(o_s)
        sp_first.append(f_s)
        inblock.append(i_s)
    weights, new_carries = [], []
    for o_s, f_s, i_s, carry in zip(own, sp_first, inblock, carries):
        ws = []
        for o, f, l, m, r0 in zip(o_s, f_s, i_s, masks, row0s):
            later = l + carry[r0:]
            a = jnp.exp2(o + later)
            ws.append(a if m is None else jnp.where(m, a, 0.0))
            new = first(later) - f
            carry = new if r0 == 0 else jnp.concatenate([carry[:r0], new], axis=0)
        weights.append(ws)
        new_carries.append(carry)
    return weights, tuple(new_carries)


def _sb_prompt_kernel(bias_ref, q_ref, k_ref, v_ref, o_ref, *rest, emit_kv_t):
    p = pl.program_id(1)
    i = pl.program_id(2)
    blk = SB_BLOCK
    nh = LANES // HD_B
    if emit_kv_t:
        kt_ref, vt_ref = rest[:2]
        rest = rest[2:]
    k_scs, v_scs = rest[:nh], rest[nh:]
    spare = [((hh + 1) % nh) * HD_B for hh in range(nh)]

    @pl.when(i == 0)
    def _():
        k = k_ref[...]
        v = v_ref[...]
        lane_id = lax.broadcasted_iota(jnp.int32, k.shape, 1)
        for hh in range(nh):
            ones = (lane_id == spare[hh]) | (lane_id == spare[hh] + 1)
            k_scs[hh][...] = jnp.where(lane_id // HD_B == hh, k, jnp.where(ones, 1.0, 0.0)).astype(BF16)
            v_scs[hh][...] = jnp.where(lane_id // HD_B == hh, v, 0.0).astype(BF16)
        if emit_kv_t:
            kt_ref[0] = k.T
            vt_ref[0] = v.T

    q = q_ref[...]
    tq = q.shape[0]
    per_q = tq // blk
    lane_id = lax.broadcasted_iota(jnp.int32, q.shape, 1)
    later_mat = _neg_later(blk, keys_on_rows=False)
    qh = []
    for hh in range(nh):
        b2 = jnp.full((1, LANES), bias_ref[p * nh + hh] * LOG2E, F32)
        b2_hi = b2.astype(BF16).astype(F32)
        extra = jnp.where(lane_id == spare[hh], b2_hi, jnp.where(lane_id == spare[hh] + 1, b2 - b2_hi, 0.0))
        qh.append(jnp.where(lane_id // HD_B == hh, q * (HD_B ** -0.5 * LOG2E), extra).astype(BF16))

    def tiles(js, masks, row0s, carries, acc):
        offs = [pl.multiple_of(j * blk, blk) for j in js]
        zs = [[lax.dot_general(qh[hh][r0:], k_scs[hh][pl.ds(off, blk), :], (((1,), (1,)), ((), ())),
                               preferred_element_type=F32) for off, r0 in zip(offs, row0s)] for hh in range(nh)]
        ws, carries = _sb_weights(zs, masks, later_mat, carries, row0s=row0s)
        for hh in range(nh):
            for off, a, r0 in zip(offs, ws[hh], row0s):
                upd = jnp.dot(a.astype(BF16), v_scs[hh][pl.ds(off, blk), :], preferred_element_type=F32)
                acc = acc + (upd if r0 == 0 else jnp.concatenate([jnp.zeros((r0, LANES), F32), upd], axis=0))
        return carries, acc

    rel = lax.broadcasted_iota(jnp.int32, (tq, blk), 0) - lax.broadcasted_iota(jnp.int32, (tq, blk), 1)
    first = i * per_q
    order = list(reversed(range(per_q)))
    zero_c = tuple(jnp.zeros((tq, 1), F32) for _ in range(nh))
    carries, acc = tiles([first + t for t in order], [rel[t * blk:] > t * blk for t in order],
                         [t * blk for t in order], zero_c, jnp.zeros(q.shape, F32))

    def body(t, state):
        j = first - 1 - t * per_q
        return tiles([j - u for u in range(per_q)], [None] * per_q, [0] * per_q, state[0], state[1])

    _, acc = lax.fori_loop(0, i, body, (carries, acc))
    o_ref[...] = acc.astype(o_ref.dtype)


def _sb_prompt(qarr, kvarr, bias, nb, seq, emit_kv_t):
    blk = SB_QBLOCK
    npair = D_MAIN // LANES
    ni = seq // blk
    out_specs = [pl.BlockSpec((blk, LANES), lambda b, p, i, s: (b * ni + i, p))]
    out_shape = [jax.ShapeDtypeStruct((nb * seq, D_MAIN), BF16)]
    if emit_kv_t:
        out_specs += [pl.BlockSpec((1, LANES, seq), lambda b, p, i, s: (b, p, 0))] * 2
        out_shape += [jax.ShapeDtypeStruct((nb, D_MAIN, seq), F32)] * 2
    grid_spec = pltpu.PrefetchScalarGridSpec(
        num_scalar_prefetch=1,
        grid=(nb, npair, ni),
        in_specs=[
            pl.BlockSpec((blk, LANES), lambda b, p, i, s: (b * ni + i, p)),
            pl.BlockSpec((seq, LANES), lambda b, p, i, s: (b, p)),
            pl.BlockSpec((seq, LANES), lambda b, p, i, s: (b, npair + p)),
        ],
        out_specs=out_specs,
        scratch_shapes=[pltpu.VMEM((seq, LANES), BF16)] * (2 * LANES // HD_B),
    )
    return pl.pallas_call(
        functools.partial(_sb_prompt_kernel, emit_kv_t=emit_kv_t),
        grid_spec=grid_spec,
        out_shape=out_shape,
        compiler_params=_cparams(("parallel", "parallel", "arbitrary")),
        name="sb_prompt",
    )(bias, qarr, kvarr, kvarr)


def _sb_sample_kernel(pt_ref, q_ref, knew_ref, vnew_ref, *refs, lq, page, npage):
    kt_refs, vt_refs = refs[:npage], refs[npage:2 * npage]
    bias_ref, o_ref, pad_sc = refs[2 * npage:]
    rows = H_B * lq
    later_mat = _neg_later(page, keys_on_rows=True)
    tn_dims = (((0,), (0,)), ((), ()))

    q = q_ref[...]
    lane = lax.broadcasted_iota(jnp.int32, q.shape, 1) // HD_B
    qx = jnp.concatenate([jnp.where(lane == h, q, 0.0) for h in range(H_B)]
                         + [jnp.zeros((page - rows, D_MAIN), F32)], axis=0)
    qxt = (qx * (HD_B ** -0.5 * LOG2E)).T.astype(BF16)
    bias = bias_ref[...] * LOG2E

    pad_sc[...] = jnp.zeros_like(pad_sc)
    pad_sc[0:lq, :] = knew_ref[...]
    knew = pad_sc[...].astype(BF16)
    pad_sc[0:lq, :] = vnew_ref[...]
    vnew = pad_sc[...].astype(BF16)
    kidx = lax.broadcasted_iota(jnp.int32, (page, page), 0)
    qidx = lax.broadcasted_iota(jnp.int32, (page, page), 1) % lq
    order = list(reversed(range(npage)))

    zs = [jnp.dot(knew, qxt, preferred_element_type=F32) + bias]
    zs += [lax.dot_general(kt_refs[g][0].astype(BF16), qxt, tn_dims, preferred_element_type=F32) + bias
           for g in order]
    masks = [kidx < qidx] + [None] * npage
    (ws,), _ = _sb_weights([zs], masks, later_mat, (jnp.zeros((1, page), F32),), keys_on_rows=True)

    acc = lax.dot_general(vnew, ws[0].astype(BF16), tn_dims, preferred_element_type=F32)
    for g, a in zip(order, ws[1:]):
        acc = acc + jnp.dot(vt_refs[g][0].astype(BF16), a.astype(BF16), preferred_element_type=F32)

    acc_t = acc.T
    olane = lax.broadcasted_iota(jnp.int32, (lq, D_MAIN), 1) // HD_B
    out = jnp.zeros((lq, D_MAIN), F32)
    for h in range(H_B):
        out = out + jnp.where(olane == h, acc_t[h * lq:(h + 1) * lq, :], 0.0)
    o_ref[...] = out


def _sb_sample(qarr, kvarr, row0, nb, lq, cache_kt, cache_vt, page_table, bias_cols):
    npage = page_table.shape[1]
    page = cache_kt.shape[2]
    rb0 = row0 // lq
    page_specs = [pl.BlockSpec((1, D_MAIN, page), lambda b, pt, g=g: (pt[b, g], 0, 0)) for g in range(npage)]
    grid_spec = pltpu.PrefetchScalarGridSpec(
        num_scalar_prefetch=1,
        grid=(nb,),
        in_specs=[
            pl.BlockSpec((lq, D_MAIN), lambda b, pt: (rb0 + b, 0)),
            pl.BlockSpec((lq, D_MAIN), lambda b, pt: (rb0 + b, 0)),
            pl.BlockSpec((lq, D_MAIN), lambda b, pt: (rb0 + b, 1)),
        ] + page_specs + page_specs + [pl.BlockSpec((1, page), lambda b, pt: (0, 0))],
        out_specs=pl.BlockSpec((lq, D_MAIN), lambda b, pt: (b, 0)),
        scratch_shapes=[pltpu.VMEM((page, D_MAIN), F32)],
    )
    return pl.pallas_call(
        functools.partial(_sb_sample_kernel, lq=lq, page=page, npage=npage),
        grid_spec=grid_spec,
        out_shape=jax.ShapeDtypeStruct((nb * lq, D_MAIN), F32),
        compiler_params=_cparams(("parallel",)),
        name="sb_sample",
    )(page_table, qarr, kvarr, kvarr, *([cache_kt] * npage), *([cache_vt] * npage), bias_cols)


def _pad_heads(w):
    lead = w.shape[:-1]
    w = w.reshape(lead + (H_A, HD_A))
    w = jnp.pad(w, [(0, 0)] * len(lead) + [(0, 0), (0, HD_A_PAD - HD_A)])
    return w.reshape(lead + (D_A_PAD,))


def _unpad_heads(w, nh):
    lead = w.shape[:-1]
    return w.reshape(lead + (nh, HD_A_PAD))[..., :HD_A].reshape(lead + (nh * HD_A,))


def _ffn_weights(w_up, w_down):
    nchunk = D_FF // FF_CHUNK
    wg = w_up[:, :D_FF].reshape(D_MODEL, nchunk, FF_CHUNK)
    wu = w_up[:, D_FF:].reshape(D_MODEL, nchunk, FF_CHUNK)
    wup_r = jnp.concatenate([wg, wu], axis=-1).transpose(1, 0, 2).astype(BF16)
    return wup_r, w_down.astype(BF16)


def kernel(x_prompt, x_sample, cache_k, cache_v, cache_mem_k, cache_mem_v, state_C, state_n, state_m, state_conv, page_table, mem_prompt, g_ffn1, w_ffn1_up, w_ffn1_down, g_mix, w_in_a, b_gate_a, conv_w_a, conv_b_a, g_h_a, g_kv, w_kv, g_k_b, w_in_b, g_q_b, b_sb, g_mem, w_mem_kv, g_q_mem, g_k_mem, w_out, g_ffn2, w_ffn2_up, w_ffn2_down):
    bp, seq, d = x_prompt.shape
    bs, lq, _ = x_sample.shape
    depth = g_ffn1.shape[0]
    n_a = w_in_a.shape[0]
    tp, ts = bp * seq, bs * lq
    mem_len = mem_prompt.shape[1]
    n_phys, page = cache_k.shape[0], cache_k.shape[1]

    x = jnp.concatenate([x_prompt.reshape(tp, d), x_sample.reshape(bs * lq, d)], axis=0)
    ones_d = jnp.ones((d,), F32)

    mem_rows = mem_prompt.reshape(bp * mem_len, d)
    mem_kv_p = []
    for l in range(depth):
        gain = jnp.concatenate([jnp.tile(g_k_mem[l], H_MEM), jnp.ones((D_MEM,), F32)])
        mem_kv_p.append(_proj(mem_rows, g_mem[l], w_mem_kv[l].astype(BF16), gain, (0,), tn=2 * D_MEM,
                              tm=bp * mem_len, head_width=HD_MEM))
    cmkt = cache_mem_k.transpose(0, 1, 3, 4, 2).reshape(depth * bs * D_MEM, mem_len)
    cmvt = cache_mem_v.transpose(0, 1, 3, 4, 2).reshape(depth * bs * D_MEM, mem_len)
    ckt = cache_k.transpose(0, 2, 3, 1).reshape(n_phys, D_MAIN, page)
    cvt = cache_v.transpose(0, 2, 3, 1).reshape(n_phys, D_MAIN, page)

    convs_p, convs_s, ns_p, ns_s, ms_p, ms_s = [], [], [], [], [], []
    c_p = jnp.zeros((n_a, bp, H_A, HD_A, HD_A), F32)
    c_s = jnp.zeros((n_a, bs, H_A, HD_A, HD_A), F32)
    c0_p = jnp.zeros((1, bp, H_A, HD_A, HD_A), F32)
    kv = None
    for l in range(depth):
        if l == n_a:
            gain = jnp.concatenate([jnp.tile(g_k_b, H_B), jnp.ones((D_MAIN,), F32)])
            kv = _proj(x, g_kv, w_kv.astype(BF16), gain, tuple(range(D_MAIN // GROUP)), tn=2 * D_MAIN)
        x = _ffn(x, g_ffn1[l], *_ffn_weights(w_ffn1_up[l], w_ffn1_down[l]))
        wo = w_out[l]
        if l < n_a:
            wi = w_in_a[l]
            q_w, k_w = wi[:, :D_MAIN], wi[:, D_MAIN:2 * D_MAIN]
            v_w, o_w = wi[:, 2 * D_MAIN:3 * D_MAIN], wi[:, 3 * D_MAIN:4 * D_MAIN]
            gate_w = wi[:, 4 * D_MAIN:4 * D_MAIN + 2 * H_A]
            qm_w = wi[:, 4 * D_MAIN + 2 * H_A:]
            w_a = jnp.concatenate([_pad_heads(q_w), _pad_heads(k_w), _pad_heads(v_w), _pad_heads(o_w), qm_w,
                                   jnp.pad(gate_w, ((0, 0), (0, GROUP - 2 * H_A)))], axis=1).astype(BF16)
            n_cols = w_a.shape[1]
            gain = jnp.concatenate([jnp.ones((4 * D_A_PAD,), F32), jnp.tile(g_q_mem[l], H_MEM),
                                    jnp.ones((GROUP,), F32)])
            proj = _proj(x, g_mix[l], w_a, gain, (4 * D_A_PAD // GROUP,), tn=6 * GROUP, head_width=HD_MEM)
            qm_col = 4 * D_A_PAD // D_MEM

            convw = jnp.concatenate([_pad_heads(conv_w_a[l][:, :D_MAIN]), _pad_heads(conv_w_a[l][:, D_MAIN:])], axis=1)
            convb = jnp.concatenate([_pad_heads(conv_b_a[l][:D_MAIN]), _pad_heads(conv_b_a[l][D_MAIN:])]).reshape(1, -1)
            bgate = jnp.pad(b_gate_a[l], (0, GROUP - 2 * H_A)).reshape(1, GROUP)
            gh = _pad_heads(g_h_a[l].reshape(-1)).reshape(1, D_A_PAD)

            def conv_state(c):
                c = jnp.concatenate([_pad_heads(c[..., :D_MAIN]), _pad_heads(c[..., D_MAIN:])], axis=-1)
                return jnp.pad(c, ((0, 0), (8 - (CONV_W - 1), 0), (0, 0)))

            main_p, c_p, n_p, m_p = _mlstm(
                proj, 0, bp, seq, SB_BLOCK, SB_BLOCK,
                conv_state(jnp.zeros((bp, CONV_W - 1, 2 * D_MAIN), F32)), c0_p, 0, c_p, l,
                jnp.zeros((bp, H_A, HD_A), F32), jnp.zeros((bp, H_A, LANES), F32),
                convw, convb, bgate, gh, BF16)
            main_s, c_s, n_s, m_s = _mlstm(
                proj, tp, bs, lq, lq, LANES, conv_state(state_conv[l]), state_C, l, c_s, l, state_n[l],
                jnp.broadcast_to(state_m[l][..., None], (bs, H_A, LANES)), convw, convb, bgate, gh, F32)
            w_main = _pad_heads(wo[:D_MAIN].T).T.astype(BF16)

            tail = CONV_W - 1
            conv_rows_p = jnp.stack([proj[(b + 1) * seq - tail:(b + 1) * seq, :2 * D_A_PAD] for b in range(bp)])
            conv_rows_s = proj[tp:].reshape(bs, lq, -1)[:, lq - tail:, :2 * D_A_PAD]
            convs_p.append(_unpad_heads(conv_rows_p, 2 * H_A))
            convs_s.append(_unpad_heads(conv_rows_s, 2 * H_A))
            ns_p.append(n_p); ms_p.append(m_p[..., 0])
            ns_s.append(n_s); ms_s.append(m_s[..., 0])
        else:
            j = l - n_a
            gain = jnp.concatenate([jnp.tile(g_q_b[j], H_B), jnp.tile(g_q_mem[l], H_MEM)])
            proj = _proj(x, g_mix[l], w_in_b[j].astype(BF16), gain, tuple(range(D_MODEL // GROUP)), tn=4 * GROUP)
            qm_col = D_MAIN // D_MEM
            sb_out = _sb_prompt(proj, kv, b_sb[j], bp, seq, emit_kv_t=(j == 0))
            main_p = sb_out[0]
            if j == 0:
                k_prompt, v_prompt = [a.reshape(bp, H_B, HD_B, seq).transpose(0, 3, 1, 2) for a in sb_out[1:]]
            bias_cols = jnp.pad(jnp.repeat(b_sb[j], lq), (0, page - H_B * lq)).reshape(1, page)
            main_s = _sb_sample(proj, kv, tp, bs, lq, ckt, cvt, page_table, bias_cols)
            w_main = wo[:D_MAIN].astype(BF16)

        mem_p = _memattn(proj, qm_col, 0, bp, seq, 512, mem_kv_p[l], 0, mem_kv_p[l], 1, BF16)
        mem_s = _memattn_small(proj, qm_col, tp, bs, lq, mem_len, cmkt, cmvt, l * bs)
        w_mem = wo[D_MAIN:].astype(BF16)
        x = _outproj(x, 0, main_p, w_main, mem_p, w_mem)
        x = _outproj(x, tp, main_s, w_main, mem_s, w_mem)
        ffn2 = _ffn_weights(w_ffn2_up[l], w_ffn2_down[l])
        if l < depth - 1:
            x = _ffn(x, g_ffn2[l], *ffn2)
        else:
            y_p = _ffn(x, g_ffn2[l], *ffn2, row0=0, nrows=tp)
            y_s = _ffn(x, g_ffn2[l], *ffn2, row0=tp, nrows=ts)

    mem_k_prompt = jnp.stack([m[:, :D_MEM] for m in mem_kv_p]).reshape(depth, bp, mem_len, H_MEM, HD_MEM)
    mem_v_prompt = jnp.stack([m[:, D_MEM:] for m in mem_kv_p]).reshape(depth, bp, mem_len, H_MEM, HD_MEM)
    return (y_p.reshape(bp, seq, d), y_s.reshape(bs, lq, d),
            k_prompt, v_prompt,
            kv[tp:, :D_MAIN].reshape(bs, lq, H_B, HD_B), kv[tp:, D_MAIN:].reshape(bs, lq, H_B, HD_B),
            mem_k_prompt, mem_v_prompt,
            jnp.stack(convs_p), c_p, jnp.stack(ns_p), jnp.stack(ms_p),
            jnp.stack(convs_s), c_s, jnp.stack(ns_s), jnp.stack(ms_s))
```

```python
import functools

import jax
import jax.numpy as jnp
from jax import lax
from jax.experimental import pallas as pl
from jax.experimental.pallas import tpu as pltpu

F32 = jnp.float32
BF16 = jnp.bfloat16

EPS = 1e-6
D_MODEL = 1024
H_MEM, HD_MEM = 4, 64
D_MEM = H_MEM * HD_MEM
D_MAIN = D_MODEL - D_MEM
H_A, HD_A = 4, 192
H_B, HD_B = 12, 64
HD_A_PAD = 256
D_A_PAD = H_A * HD_A_PAD
CONV_W = 4
D_FF = 2816
FF_CHUNK = 256
LANES = 128
GROUP = 256
ROW_TILE = 1024
FFN_ROW_TILE = 1024
VMEM_LIMIT = 56 * 1024 * 1024


def _cparams(sem, vmem=VMEM_LIMIT):
    return pltpu.CompilerParams(dimension_semantics=sem, vmem_limit_bytes=vmem)


def _rms(x, g):
    return x * lax.rsqrt(jnp.mean(x * x, axis=-1, keepdims=True) + EPS) * g


def _sigmoid(x):
    return 1.0 / (1.0 + jnp.exp(-x))


def _softplus(x):
    return jnp.maximum(x, 0.0) + jnp.log(1.0 + jnp.exp(-jnp.abs(x)))


def _head_group_matrix(n, width):
    r = lax.broadcasted_iota(jnp.int32, (n, n), 0) // width
    c = lax.broadcasted_iota(jnp.int32, (n, n), 1) // width
    return (r == c).astype(BF16)


def _ffn_kernel(x_ref, g_ref, wup_ref, wdn_ref, o_ref, h_sc, acc_sc):
    h_sc[...] = _rms(x_ref[...], g_ref[...]).astype(BF16)
    d_ff = wdn_ref.shape[1]
    for c in range(d_ff // FF_CHUNK):
        lo, hi = c * FF_CHUNK, (c + 1) * FF_CHUNK
        g = jnp.dot(h_sc[...], wup_ref[0, :, lo:hi], preferred_element_type=F32)
        u = jnp.dot(h_sc[...], wup_ref[0, :, d_ff + lo:d_ff + hi], preferred_element_type=F32)
        act = (g * _sigmoid(g) * u).astype(BF16)
        down = jnp.dot(act, wdn_ref[0, lo:hi, :], preferred_element_type=F32)
        if c == 0:
            acc_sc[...] = down
        else:
            acc_sc[...] += down
    o_ref[...] = x_ref[...] + 0.5 * acc_sc[...]


def _ffn(x, g, w_up, w_down, layer, row0=0, nrows=None):
    t, d = x.shape
    nrows = t if nrows is None else nrows
    d_ff = w_down.shape[1]
    tm = FFN_ROW_TILE
    rb0 = row0 // tm
    resident = pl.Buffered(1)
    return pl.pallas_call(
        _ffn_kernel,
        grid=(nrows // tm,),
        in_specs=[
            pl.BlockSpec((tm, d), lambda i: (rb0 + i, 0)),
            pl.BlockSpec((1, d), lambda i: (0, 0)),
            pl.BlockSpec((1, d, 2 * d_ff), lambda i: (layer, 0, 0), pipeline_mode=resident),
            pl.BlockSpec((1, d_ff, d), lambda i: (layer, 0, 0), pipeline_mode=resident),
        ],
        out_specs=pl.BlockSpec((tm, d), lambda i: (i, 0)),
        out_shape=jax.ShapeDtypeStruct((nrows, d), F32),
        scratch_shapes=[pltpu.VMEM((tm, d), BF16), pltpu.VMEM((tm, d), F32)],
        compiler_params=_cparams(("parallel",)),
        name="ffn",
    )(x, g.reshape(1, d), w_up, w_down)


def _proj_kernel(x_ref, g_ref, w_ref, gain_ref, o_ref, h_sc, *, normed, head_width):
    j = pl.program_id(1)

    @pl.when(j == 0)
    def _():
        h_sc[...] = _rms(x_ref[...], g_ref[...]).astype(BF16)

    o_ref[...] = jnp.dot(h_sc[...], w_ref[...], preferred_element_type=F32)

    for tile, k in normed:
        @pl.when(j == tile)
        def _(k=k):
            sl = slice(k * GROUP, (k + 1) * GROUP)
            y = o_ref[:, sl]
            ss = jnp.dot((y * y).astype(BF16), _head_group_matrix(GROUP, head_width),
                         preferred_element_type=F32)
            o_ref[:, sl] = y * lax.rsqrt(ss * (1.0 / head_width) + EPS) * gain_ref[:, sl]


def _proj(x, g, w, gain, normed_groups, tn, tm=ROW_TILE, head_width=HD_B):
    t, d = x.shape
    n = w.shape[1]
    nsub = tn // GROUP
    normed = tuple((gi // nsub, gi % nsub) for gi in normed_groups)
    return pl.pallas_call(
        functools.partial(_proj_kernel, normed=normed, head_width=head_width),
        grid=(t // tm, n // tn),
        in_specs=[
            pl.BlockSpec((tm, d), lambda i, j: (i, 0)),
            pl.BlockSpec((1, d), lambda i, j: (0, 0)),
            pl.BlockSpec((d, tn), lambda i, j: (0, j)),
            pl.BlockSpec((1, tn), lambda i, j: (0, j)),
        ],
        out_specs=pl.BlockSpec((tm, tn), lambda i, j: (i, j)),
        out_shape=jax.ShapeDtypeStruct((t, n), F32),
        scratch_shapes=[pltpu.VMEM((tm, d), BF16)],
        compiler_params=_cparams(("parallel", "arbitrary")),
        name="proj",
    )(x, g.reshape(1, d), w, gain.reshape(1, n))


def _outproj_kernel(x_ref, a_ref, wa_ref, b_ref, wb_ref, o_ref):
    o_ref[...] = (x_ref[...]
                  + jnp.dot(a_ref[...].astype(BF16), wa_ref[...], preferred_element_type=F32)
                  + jnp.dot(b_ref[...].astype(BF16), wb_ref[...], preferred_element_type=F32))


def _outproj(x, row0, a, wa, b, wb):
    t, d = x.shape
    tm = ROW_TILE
    nrows, ka = a.shape
    kb = b.shape[1]
    rb0 = row0 // tm
    return pl.pallas_call(
        _outproj_kernel,
        grid=(nrows // tm,),
        in_specs=[
            pl.BlockSpec((tm, d), lambda i: (rb0 + i, 0)),
            pl.BlockSpec((tm, ka), lambda i: (i, 0)),
            pl.BlockSpec((ka, d), lambda i: (0, 0)),
            pl.BlockSpec((tm, kb), lambda i: (i, 0)),
            pl.BlockSpec((kb, d), lambda i: (0, 0)),
        ],
        out_specs=pl.BlockSpec((tm, d), lambda i: (rb0 + i, 0)),
        out_shape=jax.ShapeDtypeStruct((t, d), F32),
        input_output_aliases={0: 0},
        compiler_params=_cparams(("parallel",)),
        name="outproj",
    )(x, a, wa, b, wb)


def _memattn_kernel(q_ref, k_ref, v_ref, o_ref):
    q = q_ref[...]
    kb = k_ref[...].astype(BF16)
    vb = v_ref[...].astype(BF16)
    lane = lax.broadcasted_iota(jnp.int32, q.shape, 1) // HD_MEM
    scores = []
    for h in range(H_MEM):
        qh = jnp.where(lane == h, q, 0.0).astype(BF16)
        scores.append(lax.dot_general(qh, kb, (((1,), (1,)), ((), ())), preferred_element_type=F32)
                      * (HD_MEM ** -0.5))
    probs = []
    for s in scores:
        e = jnp.exp(s - jnp.max(s, axis=-1, keepdims=True))
        probs.append((e / jnp.sum(e, axis=-1, keepdims=True)).astype(BF16))
    acc = jnp.zeros(q.shape, F32)
    for h, p in enumerate(probs):
        acc = acc + jnp.where(lane == h, jnp.dot(p, vb, preferred_element_type=F32), 0.0)
    o_ref[...] = acc.astype(o_ref.dtype)


def _memattn(qarr, qcol, row0, nb, lq, tq, karr, kcol, varr, vcol, out_dtype):
    m = karr.shape[0] // nb
    nq = lq // tq
    rb0 = row0 // tq
    return pl.pallas_call(
        _memattn_kernel,
        grid=(nb, nq),
        in_specs=[
            pl.BlockSpec((tq, D_MEM), lambda b, i: (rb0 + b * nq + i, qcol)),
            pl.BlockSpec((m, D_MEM), lambda b, i: (b, kcol)),
            pl.BlockSpec((m, D_MEM), lambda b, i: (b, vcol)),
        ],
        out_specs=pl.BlockSpec((tq, D_MEM), lambda b, i: (b * nq + i, 0)),
        out_shape=jax.ShapeDtypeStruct((nb * lq, D_MEM), out_dtype),
        compiler_params=_cparams(("parallel", "parallel")),
        name="memattn",
    )(qarr, karr, varr)


def _memattn_small_kernel(q_ref, k_ref, v_ref, o_ref, *, nseq, lq):
    lane = lax.broadcasted_iota(jnp.int32, (lq, D_MEM), 1) // HD_MEM
    scores = []
    for s in range(nseq):
        q = q_ref[s * lq:(s + 1) * lq, :]
        qx = jnp.concatenate([jnp.where(lane == h, q, 0.0) for h in range(H_MEM)], axis=0).astype(BF16)
        kt = k_ref[s * D_MEM:(s + 1) * D_MEM, :].astype(BF16)
        scores.append(jnp.dot(qx, kt, preferred_element_type=F32) * (HD_MEM ** -0.5))
    probs = []
    for sc in scores:
        e = jnp.exp(sc - jnp.max(sc, axis=-1, keepdims=True))
        probs.append((e / jnp.sum(e, axis=-1, keepdims=True)).astype(BF16))
    for s, p in enumerate(probs):
        vt = v_ref[s * D_MEM:(s + 1) * D_MEM, :].astype(BF16)
        ox = lax.dot_general(p, vt, (((1,), (1,)), ((), ())), preferred_element_type=F32)
        out = jnp.zeros((lq, D_MEM), F32)
        for h in range(H_MEM):
            out = out + jnp.where(lane == h, ox[h * lq:(h + 1) * lq, :], 0.0)
        o_ref[s * lq:(s + 1) * lq, :] = out


def _memattn_small(qarr, qcol, row0, nb, lq, m, ktarr, vtarr, seq0, nseq=8):
    rb0 = row0 // (nseq * lq)
    kb0 = seq0 // nseq
    return pl.pallas_call(
        functools.partial(_memattn_small_kernel, nseq=nseq, lq=lq),
        grid=(nb // nseq,),
        in_specs=[
            pl.BlockSpec((nseq * lq, D_MEM), lambda i: (rb0 + i, qcol)),
            pl.BlockSpec((nseq * D_MEM, m), lambda i: (kb0 + i, 0)),
            pl.BlockSpec((nseq * D_MEM, m), lambda i: (kb0 + i, 0)),
        ],
        out_specs=pl.BlockSpec((nseq * lq, D_MEM), lambda i: (i, 0)),
        out_shape=jax.ShapeDtypeStruct((nb * lq, D_MEM), F32),
        compiler_params=_cparams(("parallel",)),
        name="memattn_small",
    )(qarr, ktarr, vtarr)


def _mlstm_kernel(qkvo_ref, gates_ref, convprev_ref, c0_ref, n0_ref, m0_ref,
                  convw_ref, convb_ref, bgate_ref, gh_ref, c_all_ref,
                  h_ref, c_out_ref, n_out_ref, m_out_ref,
                  ext_sc, c_sc, n_sc, m_sc, *stage, lb, lp):
    ci = pl.program_id(1)
    last = lb - 1

    @pl.when(ci == 0)
    def _():
        ext_sc[0:8, :] = convprev_ref[0]
        c_sc[...] = jnp.zeros_like(c_sc)
        n_sc[...] = jnp.zeros_like(n_sc)
        for h in range(H_A):
            c_sc[h, 0:HD_A, 0:HD_A] = c0_ref[0, 0, h]
        n_sc[0:H_A, 0:HD_A] = n0_ref[0]
        m_sc[0:H_A, :] = m0_ref[0]

    if lb == lp:
        src, gsrc = qkvo_ref, gates_ref
    else:
        src, gsrc = stage
        src[...] = jnp.zeros_like(src)
        gsrc[...] = jnp.zeros_like(gsrc)
        src[0:lb, :] = qkvo_ref[...]
        gsrc[0:lb, :] = gates_ref[...]

    ext_sc[8:8 + lp, :] = src[:, 0:2 * D_A_PAD]

    row = lax.broadcasted_iota(jnp.int32, (lp, lp), 0)
    col = lax.broadcasted_iota(jnp.int32, (lp, lp), 1)
    causal = row >= col
    valid_row = lax.broadcasted_iota(jnp.int32, (lp, 1), 0) < lb

    gl = lax.broadcasted_iota(jnp.int32, (lp, GROUP), 1)
    gts = gsrc[...] + bgate_ref[...]
    gts = jnp.where((gl >= H_A) & (gl < 2 * H_A), -_softplus(-gts), gts)
    gts = jnp.where(valid_row & (gl < 2 * H_A), gts, 0.0)
    cum = jnp.dot(causal.astype(F32), gts, preferred_element_type=F32, precision=lax.Precision.HIGHEST)
    gts_t = gts.T
    cum_t = cum.T

    def conv_silu(c0):
        sl = slice(c0, c0 + HD_A_PAD)
        a = convb_ref[:, sl]
        for j in range(CONV_W):
            a = a + ext_sc[pl.ds(8 - (CONV_W - 1) + j, lp), sl] * convw_ref[j:j + 1, sl]
        a = a * _sigmoid(a)
        return a if lb == lp else jnp.where(valid_row, a, 0.0)

    heads = []
    for h in range(H_A):
        q = conv_silu(h * HD_A_PAD)
        k = conv_silu(D_A_PAD + h * HD_A_PAD) * (HD_A ** -0.5)
        vb = src[:, 2 * D_A_PAD + h * HD_A_PAD:2 * D_A_PAD + (h + 1) * HD_A_PAD].astype(BF16)
        qb, kb = q.astype(BF16), k.astype(BF16)
        s = lax.dot_general(qb, kb, (((1,), (1,)), ((), ())), preferred_element_type=F32)
        cq = lax.dot_general(qb, c_sc[h].astype(BF16), (((1,), (1,)), ((), ())), preferred_element_type=F32)
        qn = jnp.sum(q * n_sc[h:h + 1, :], axis=-1, keepdims=True)
        heads.append((k, vb, s, cq, qn))

    for h, (k, vb, s, cq, qn) in enumerate(heads):
        hs = slice(h * HD_A_PAD, (h + 1) * HD_A_PAD)
        o_pre = src[:, 3 * D_A_PAD + h * HD_A_PAD:3 * D_A_PAD + (h + 1) * HD_A_PAD]

        ig_col = gts[:, h:h + 1]
        b_col = cum[:, H_A + h:H_A + h + 1]
        ig_row = gts_t[h:h + 1, :]
        b_row = cum_t[H_A + h:H_A + h + 1, :]
        m_prev = m_sc[h:h + 1, 0:1]
        c_prev = c_sc[h]
        n_prev = n_sc[h:h + 1, :]

        dlog = jnp.where(causal, b_col - b_row + ig_row, -jnp.inf)
        inter = b_col + m_prev
        m_t = jnp.maximum(inter, jnp.max(dlog, axis=-1, keepdims=True))
        w = jnp.exp(dlog - m_t) * s
        w_inter = jnp.exp(inter - m_t)
        num = jnp.dot(w.astype(BF16), vb, preferred_element_type=F32) + cq * w_inter
        den = jnp.sum(w, axis=-1, keepdims=True) + w_inter * qn
        den = jnp.maximum(jnp.abs(den), jnp.exp(-m_t))
        hh = num / den
        hn = hh * lax.rsqrt(jnp.sum(hh * hh, axis=-1, keepdims=True) * (1.0 / HD_A) + EPS) * gh_ref[:, hs]
        out = hn * _sigmoid(o_pre)
        h_ref[:, hs] = out[0:lb, :].astype(h_ref.dtype)

        m_new = m_t[last:last + 1, :]
        b_last = b_col[last:last + 1, :]
        w_end = jnp.exp(b_last - b_col + ig_col - m_new)
        decay = jnp.exp(b_last + m_prev - m_new)
        kw = k * w_end
        c_sc[h] = decay * c_prev + lax.dot_general(vb, kw.astype(BF16), (((0,), (0,)), ((), ())),
                                                   preferred_element_type=F32)
        n_sc[h:h + 1, :] = decay * n_prev + jnp.sum(kw, axis=0, keepdims=True)
        m_sc[h:h + 1, :] = jnp.broadcast_to(m_new, (1, LANES))

    ext_sc[0:8, :] = ext_sc[lp:lp + 8, :]

    @pl.when(ci == pl.num_programs(1) - 1)
    def _():
        for h in range(H_A):
            c_out_ref[0, 0, h] = c_sc[h, 0:HD_A, 0:HD_A]
        n_out_ref[0] = n_sc[0:H_A, 0:HD_A]
        m_out_ref[0] = m_sc[0:H_A, :]


def _mlstm(proj, row0, nb, seq, lb, lp, convprev, c0, c0_slot, c_out_prev, layer, n0, m0,
           convw, convb, bgate, gh, out_dtype):
    n_layers = c_out_prev.shape[0]
    nc = seq // lb
    rb0 = row0 // lb
    gate_col = (4 * D_A_PAD + D_MEM) // GROUP
    scratch = [
        pltpu.VMEM((lp + 8, 2 * D_A_PAD), F32),
        pltpu.VMEM((H_A, HD_A_PAD, HD_A_PAD), F32),
        pltpu.VMEM((8, HD_A_PAD), F32),
        pltpu.VMEM((8, LANES), F32),
    ]
    if lb != lp:
        scratch += [pltpu.VMEM((lp, 4 * D_A_PAD), F32), pltpu.VMEM((lp, GROUP), F32)]
    in_specs = [
        pl.BlockSpec((lb, 4 * D_A_PAD), lambda b, c: (rb0 + b * nc + c, 0)),
        pl.BlockSpec((lb, GROUP), lambda b, c: (rb0 + b * nc + c, gate_col)),
        pl.BlockSpec((1, 8, 2 * D_A_PAD), lambda b, c: (b, 0, 0)),
        pl.BlockSpec((1, 1, H_A, HD_A, HD_A), lambda b, c: (c0_slot, b, 0, 0, 0)),
        pl.BlockSpec((1, H_A, HD_A), lambda b, c: (b, 0, 0)),
        pl.BlockSpec((1, H_A, LANES), lambda b, c: (b, 0, 0)),
        pl.BlockSpec((CONV_W, 2 * D_A_PAD), lambda b, c: (0, 0)),
        pl.BlockSpec((1, 2 * D_A_PAD), lambda b, c: (0, 0)),
        pl.BlockSpec((1, GROUP), lambda b, c: (0, 0)),
        pl.BlockSpec((1, D_A_PAD), lambda b, c: (0, 0)),
    ]
    in_specs.append(pl.BlockSpec(memory_space=pl.ANY))
    args = [proj, proj, convprev, c0, n0, m0, convw, convb, bgate, gh, c_out_prev]
    aliases = {len(args) - 1: 1}
    return pl.pallas_call(
        functools.partial(_mlstm_kernel, lb=lb, lp=lp),
        grid=(nb, nc),
        in_specs=in_specs,
        out_specs=[
            pl.BlockSpec((lb, D_A_PAD), lambda b, c: (b * nc + c, 0)),
            pl.BlockSpec((1, 1, H_A, HD_A, HD_A), lambda b, c: (layer, b, 0, 0, 0)),
            pl.BlockSpec((1, H_A, HD_A), lambda b, c: (b, 0, 0)),
            pl.BlockSpec((1, H_A, LANES), lambda b, c: (b, 0, 0)),
        ],
        out_shape=[
            jax.ShapeDtypeStruct((nb * seq, D_A_PAD), out_dtype),
            jax.ShapeDtypeStruct((n_layers, nb, H_A, HD_A, HD_A), F32),
            jax.ShapeDtypeStruct((nb, H_A, HD_A), F32),
            jax.ShapeDtypeStruct((nb, H_A, LANES), F32),
        ],
        scratch_shapes=scratch,
        input_output_aliases=aliases,
        compiler_params=_cparams(("parallel", "arbitrary")),
        name="mlstm",
    )(*args)


SB_BLOCK = 256
SB_QBLOCK = 1024
LOG2E = 1.4426950408889634


def _neg_later(n, keys_on_rows):
    r = lax.broadcasted_iota(jnp.int32, (n, n), 0)
    c = lax.broadcasted_iota(jnp.int32, (n, n), 1)
    later = (c > r) if keys_on_rows else (r > c)
    return jnp.where(later, -1.0, 0.0).astype(BF16)


def _softplus2(z):
    return jnp.where(z > 64.0, z, jnp.log(1.0 + jnp.exp2(z)) * LOG2E)


def _sb_weights(z_streams, masks, later_mat, carries, keys_on_rows=False, row0s=None):
    first = (lambda x: x[0:1, :]) if keys_on_rows else (lambda x: x[:, 0:1])
    row0s = [0] * len(masks) if row0s is None else row0s

    own, sp_first, inblock = [], [], []
    for zs in z_streams:
        o_s, f_s, i_s = [], [], []
        for z, m in zip(zs, masks):
            sp = _softplus2(z) if m is None else jnp.where(m, _softplus2(z), 0.0)
            o_s.append(z - sp)
            f_s.append(first(sp))
            spb = sp.astype(BF16)
            i_s.append(jnp.dot(later_mat, spb, preferred_element_type=F32) if keys_on_rows
                       else jnp.dot(spb, later_mat, preferred_element_type=F32))
        own.append(o_s)
        sp_first.append(f_s)
        inblock.append(i_s)
    weights, new_carries = [], []
    for o_s, f_s, i_s, carry in zip(own, sp_first, inblock, carries):
        ws = []
        for o, f, l, m, r0 in zip(o_s, f_s, i_s, masks, row0s):
            later = l + carry[r0:]
            a = jnp.exp2(o + later)
            ws.append(a if m is None else jnp.where(m, a, 0.0))
            new = first(later) - f
            carry = new if r0 == 0 else jnp.concatenate([carry[:r0], new], axis=0)
        weights.append(ws)
        new_carries.append(carry)
    return weights, tuple(new_carries)


def _sb_prompt_kernel(bias_ref, q_ref, k_ref, v_ref, o_ref, *rest, emit_kv_t):
    p = pl.program_id(1)
    i = pl.program_id(2)
    blk = SB_BLOCK
    nh = LANES // HD_B
    if emit_kv_t:
        kt_ref, vt_ref = rest[:2]
        rest = rest[2:]
    k_scs, v_scs = rest[:nh], rest[nh:]
    spare = [((hh + 1) % nh) * HD_B for hh in range(nh)]

    @pl.when(i == 0)
    def _():
        k = k_ref[...]
        v = v_ref[...]
        lane_id = lax.broadcasted_iota(jnp.int32, k.shape, 1)
        for hh in range(nh):
            ones = (lane_id == spare[hh]) | (lane_id == spare[hh] + 1)
            k_scs[hh][...] = jnp.where(lane_id // HD_B == hh, k, jnp.where(ones, 1.0, 0.0)).astype(BF16)
            v_scs[hh][...] = jnp.where(lane_id // HD_B == hh, v, 0.0).astype(BF16)
        if emit_kv_t:
            kt_ref[0] = k.T
            vt_ref[0] = v.T

    q = q_ref[...]
    tq = q.shape[0]
    per_q = tq // blk
    lane_id = lax.broadcasted_iota(jnp.int32, q.shape, 1)
    later_mat = _neg_later(blk, keys_on_rows=False)
    qh = []
    for hh in range(nh):
        b2 = jnp.full((1, LANES), bias_ref[p * nh + hh] * LOG2E, F32)
        b2_hi = b2.astype(BF16).astype(F32)
        extra = jnp.where(lane_id == spare[hh], b2_hi, jnp.where(lane_id == spare[hh] + 1, b2 - b2_hi, 0.0))
        qh.append(jnp.where(lane_id // HD_B == hh, q * (HD_B ** -0.5 * LOG2E), extra).astype(BF16))

    def tiles(js, masks, row0s, carries, acc):
        offs = [pl.multiple_of(j * blk, blk) for j in js]
        zs = [[lax.dot_general(qh[hh][r0:], k_scs[hh][pl.ds(off, blk), :], (((1,), (1,)), ((), ())),
                               preferred_element_type=F32) for off, r0 in zip(offs, row0s)] for hh in range(nh)]
        ws, carries = _sb_weights(zs, masks, later_mat, carries, row0s=row0s)
        for hh in range(nh):
            for off, a, r0 in zip(offs, ws[hh], row0s):
                upd = jnp.dot(a.astype(BF16), v_scs[hh][pl.ds(off, blk), :], preferred_element_type=F32)
                acc = acc + (upd if r0 == 0 else jnp.concatenate([jnp.zeros((r0, LANES), F32), upd], axis=0))
        return carries, acc

    rel = lax.broadcasted_iota(jnp.int32, (tq, blk), 0) - lax.broadcasted_iota(jnp.int32, (tq, blk), 1)
    first = i * per_q
    order = list(reversed(range(per_q)))
    zero_c = tuple(jnp.zeros((tq, 1), F32) for _ in range(nh))
    carries, acc = tiles([first + t for t in order], [rel[t * blk:] > t * blk for t in order],
                         [t * blk for t in order], zero_c, jnp.zeros(q.shape, F32))

    def body(t, state):
        j = first - 1 - t * per_q
        return tiles([j - u for u in range(per_q)], [None] * per_q, [0] * per_q, state[0], state[1])

    _, acc = lax.fori_loop(0, i, body, (carries, acc))
    o_ref[...] = acc.astype(o_ref.dtype)


def _sb_prompt(qarr, kvarr, bias, nb, seq, emit_kv_t):
    blk = SB_QBLOCK
    npair = D_MAIN // LANES
    ni = seq // blk
    out_specs = [pl.BlockSpec((blk, LANES), lambda b, p, i, s: (b * ni + i, p))]
    out_shape = [jax.ShapeDtypeStruct((nb * seq, D_MAIN), BF16)]
    if emit_kv_t:
        out_specs += [pl.BlockSpec((1, LANES, seq), lambda b, p, i, s: (b, p, 0))] * 2
        out_shape += [jax.ShapeDtypeStruct((nb, D_MAIN, seq), F32)] * 2
    grid_spec = pltpu.PrefetchScalarGridSpec(
        num_scalar_prefetch=1,
        grid=(nb, npair, ni),
        in_specs=[
            pl.BlockSpec((blk, LANES), lambda b, p, i, s: (b * ni + i, p)),
            pl.BlockSpec((seq, LANES), lambda b, p, i, s: (b, p)),
            pl.BlockSpec((seq, LANES), lambda b, p, i, s: (b, npair + p)),
        ],
        out_specs=out_specs,
        scratch_shapes=[pltpu.VMEM((seq, LANES), BF16)] * (2 * LANES // HD_B),
    )
    return pl.pallas_call(
        functools.partial(_sb_prompt_kernel, emit_kv_t=emit_kv_t),
        grid_spec=grid_spec,
        out_shape=out_shape,
        compiler_params=_cparams(("parallel", "parallel", "arbitrary")),
        name="sb_prompt",
    )(bias, qarr, kvarr, kvarr)


def _sb_sample_kernel(pt_ref, q_ref, knew_ref, vnew_ref, *refs, lq, page, npage):
    kt_refs, vt_refs = refs[:npage], refs[npage:2 * npage]
    bias_ref, o_ref, pad_sc = refs[2 * npage:]
    rows = H_B * lq
    later_mat = _neg_later(page, keys_on_rows=True)
    tn_dims = (((0,), (0,)), ((), ()))

    q = q_ref[...]
    lane = lax.broadcasted_iota(jnp.int32, q.shape, 1) // HD_B
    qx = jnp.concatenate([jnp.where(lane == h, q, 0.0) for h in range(H_B)]
                         + [jnp.zeros((page - rows, D_MAIN), F32)], axis=0)
    qxt = (qx * (HD_B ** -0.5 * LOG2E)).T.astype(BF16)
    bias = bias_ref[...] * LOG2E

    pad_sc[...] = jnp.zeros_like(pad_sc)
    pad_sc[0:lq, :] = knew_ref[...]
    knew = pad_sc[...].astype(BF16)
    pad_sc[0:lq, :] = vnew_ref[...]
    vnew = pad_sc[...].astype(BF16)
    kidx = lax.broadcasted_iota(jnp.int32, (page, page), 0)
    qidx = lax.broadcasted_iota(jnp.int32, (page, page), 1) % lq
    order = list(reversed(range(npage)))

    zs = [jnp.dot(knew, qxt, preferred_element_type=F32) + bias]
    zs += [lax.dot_general(kt_refs[g][0].astype(BF16), qxt, tn_dims, preferred_element_type=F32) + bias
           for g in order]
    masks = [kidx < qidx] + [None] * npage
    (ws,), _ = _sb_weights([zs], masks, later_mat, (jnp.zeros((1, page), F32),), keys_on_rows=True)

    acc = lax.dot_general(vnew, ws[0].astype(BF16), tn_dims, preferred_element_type=F32)
    for g, a in zip(order, ws[1:]):
        acc = acc + jnp.dot(vt_refs[g][0].astype(BF16), a.astype(BF16), preferred_element_type=F32)

    acc_t = acc.T
    olane = lax.broadcasted_iota(jnp.int32, (lq, D_MAIN), 1) // HD_B
    out = jnp.zeros((lq, D_MAIN), F32)
    for h in range(H_B):
        out = out + jnp.where(olane == h, acc_t[h * lq:(h + 1) * lq, :], 0.0)
    o_ref[...] = out


def _sb_sample(qarr, kvarr, row0, nb, lq, cache_kt, cache_vt, page_table, bias_cols):
    npage = page_table.shape[1]
    page = cache_kt.shape[2]
    rb0 = row0 // lq
    page_specs = [pl.BlockSpec((1, D_MAIN, page), lambda b, pt, g=g: (pt[b, g], 0, 0)) for g in range(npage)]
    grid_spec = pltpu.PrefetchScalarGridSpec(
        num_scalar_prefetch=1,
        grid=(nb,),
        in_specs=[
            pl.BlockSpec((lq, D_MAIN), lambda b, pt: (rb0 + b, 0)),
            pl.BlockSpec((lq, D_MAIN), lambda b, pt: (rb0 + b, 0)),
            pl.BlockSpec((lq, D_MAIN), lambda b, pt: (rb0 + b, 1)),
        ] + page_specs + page_specs + [pl.BlockSpec((1, page), lambda b, pt: (0, 0))],
        out_specs=pl.BlockSpec((lq, D_MAIN), lambda b, pt: (b, 0)),
        scratch_shapes=[pltpu.VMEM((page, D_MAIN), F32)],
    )
    return pl.pallas_call(
        functools.partial(_sb_sample_kernel, lq=lq, page=page, npage=npage),
        grid_spec=grid_spec,
        out_shape=jax.ShapeDtypeStruct((nb * lq, D_MAIN), F32),
        compiler_params=_cparams(("parallel",)),
        name="sb_sample",
    )(page_table, qarr, kvarr, kvarr, *([cache_kt] * npage), *([cache_vt] * npage), bias_cols)


def _pad_heads(w):
    lead = w.shape[:-1]
    w = w.reshape(lead + (H_A, HD_A))
    w = jnp.pad(w, [(0, 0)] * len(lead) + [(0, 0), (0, HD_A_PAD - HD_A)])
    return w.reshape(lead + (D_A_PAD,))


def _unpad_heads(w, nh):
    lead = w.shape[:-1]
    return w.reshape(lead + (nh, HD_A_PAD))[..., :HD_A].reshape(lead + (nh * HD_A,))


def kernel(x_prompt, x_sample, cache_k, cache_v, cache_mem_k, cache_mem_v, state_C, state_n, state_m, state_conv, page_table, mem_prompt, g_ffn1, w_ffn1_up, w_ffn1_down, g_mix, w_in_a, b_gate_a, conv_w_a, conv_b_a, g_h_a, g_kv, w_kv, g_k_b, w_in_b, g_q_b, b_sb, g_mem, w_mem_kv, g_q_mem, g_k_mem, w_out, g_ffn2, w_ffn2_up, w_ffn2_down):
    bp, seq, d = x_prompt.shape
    bs, lq, _ = x_sample.shape
    depth = g_ffn1.shape[0]
    n_a = w_in_a.shape[0]
    tp, ts = bp * seq, bs * lq
    mem_len = mem_prompt.shape[1]
    n_phys, page = cache_k.shape[0], cache_k.shape[1]

    x = jnp.concatenate([x_prompt.reshape(tp, d), x_sample.reshape(bs * lq, d)], axis=0)
    wu1, wd1 = w_ffn1_up.astype(BF16), w_ffn1_down.astype(BF16)
    wu2, wd2 = w_ffn2_up.astype(BF16), w_ffn2_down.astype(BF16)

    mem_rows = mem_prompt.reshape(bp * mem_len, d)
    mem_kv_p = []
    for l in range(depth):
        gain = jnp.concatenate([jnp.tile(g_k_mem[l], H_MEM), jnp.ones((D_MEM,), F32)])
        mem_kv_p.append(_proj(mem_rows, g_mem[l], w_mem_kv[l].astype(BF16), gain, (0,), tn=2 * D_MEM,
                              tm=bp * mem_len, head_width=HD_MEM))
    cmkt = cache_mem_k.transpose(0, 1, 3, 4, 2).reshape(depth * bs * D_MEM, mem_len)
    cmvt = cache_mem_v.transpose(0, 1, 3, 4, 2).reshape(depth * bs * D_MEM, mem_len)
    ckt = cache_k.transpose(0, 2, 3, 1).reshape(n_phys, D_MAIN, page)
    cvt = cache_v.transpose(0, 2, 3, 1).reshape(n_phys, D_MAIN, page)

    convs_p, convs_s, ns_p, ns_s, ms_p, ms_s = [], [], [], [], [], []
    c_p = jnp.zeros((n_a, bp, H_A, HD_A, HD_A), F32)
    c_s = jnp.zeros((n_a, bs, H_A, HD_A, HD_A), F32)
    c0_p = jnp.zeros((1, bp, H_A, HD_A, HD_A), F32)
    kv = None
    for l in range(depth):
        if l == n_a:
            gain = jnp.concatenate([jnp.tile(g_k_b, H_B), jnp.ones((D_MAIN,), F32)])
            kv = _proj(x, g_kv, w_kv.astype(BF16), gain, tuple(range(D_MAIN // GROUP)), tn=2 * D_MAIN)
        x = _ffn(x, g_ffn1[l], wu1, wd1, l)
        wo = w_out[l]
        if l < n_a:
            wi = w_in_a[l]
            q_w, k_w = wi[:, :D_MAIN], wi[:, D_MAIN:2 * D_MAIN]
            v_w, o_w = wi[:, 2 * D_MAIN:3 * D_MAIN], wi[:, 3 * D_MAIN:4 * D_MAIN]
            gate_w = wi[:, 4 * D_MAIN:4 * D_MAIN + 2 * H_A]
            qm_w = wi[:, 4 * D_MAIN + 2 * H_A:]
            w_a = jnp.concatenate([_pad_heads(q_w), _pad_heads(k_w), _pad_heads(v_w), _pad_heads(o_w), qm_w,
                                   jnp.pad(gate_w, ((0, 0), (0, GROUP - 2 * H_A)))], axis=1).astype(BF16)
            n_cols = w_a.shape[1]
            gain = jnp.concatenate([jnp.ones((4 * D_A_PAD,), F32), jnp.tile(g_q_mem[l], H_MEM),
                                    jnp.ones((GROUP,), F32)])
            proj = _proj(x, g_mix[l], w_a, gain, (4 * D_A_PAD // GROUP,), tn=6 * GROUP, head_width=HD_MEM)
            qm_col = 4 * D_A_PAD // D_MEM

            convw = jnp.concatenate([_pad_heads(conv_w_a[l][:, :D_MAIN]), _pad_heads(conv_w_a[l][:, D_MAIN:])], axis=1)
            convb = jnp.concatenate([_pad_heads(conv_b_a[l][:D_MAIN]), _pad_heads(conv_b_a[l][D_MAIN:])]).reshape(1, -1)
            bgate = jnp.pad(b_gate_a[l], (0, GROUP - 2 * H_A)).reshape(1, GROUP)
            gh = _pad_heads(g_h_a[l].reshape(-1)).reshape(1, D_A_PAD)

            def conv_state(c):
                c = jnp.concatenate([_pad_heads(c[..., :D_MAIN]), _pad_heads(c[..., D_MAIN:])], axis=-1)
                return jnp.pad(c, ((0, 0), (8 - (CONV_W - 1), 0), (0, 0)))

            main_p, c_p, n_p, m_p = _mlstm(
                proj, 0, bp, seq, SB_BLOCK, SB_BLOCK,
                conv_state(jnp.zeros((bp, CONV_W - 1, 2 * D_MAIN), F32)), c0_p, 0, c_p, l,
                jnp.zeros((bp, H_A, HD_A), F32), jnp.zeros((bp, H_A, LANES), F32),
                convw, convb, bgate, gh, BF16)
            main_s, c_s, n_s, m_s = _mlstm(
                proj, tp, bs, lq, lq, LANES, conv_state(state_conv[l]), state_C, l, c_s, l, state_n[l],
                jnp.broadcast_to(state_m[l][..., None], (bs, H_A, LANES)), convw, convb, bgate, gh, F32)
            w_main = _pad_heads(wo[:D_MAIN].T).T.astype(BF16)

            tail = CONV_W - 1
            conv_rows_p = jnp.stack([proj[(b + 1) * seq - tail:(b + 1) * seq, :2 * D_A_PAD] for b in range(bp)])
            conv_rows_s = proj[tp:].reshape(bs, lq, -1)[:, lq - tail:, :2 * D_A_PAD]
            convs_p.append(_unpad_heads(conv_rows_p, 2 * H_A))
            convs_s.append(_unpad_heads(conv_rows_s, 2 * H_A))
            ns_p.append(n_p); ms_p.append(m_p[..., 0])
            ns_s.append(n_s); ms_s.append(m_s[..., 0])
        else:
            j = l - n_a
            gain = jnp.concatenate([jnp.tile(g_q_b[j], H_B), jnp.tile(g_q_mem[l], H_MEM)])
            proj = _proj(x, g_mix[l], w_in_b[j].astype(BF16), gain, tuple(range(D_MODEL // GROUP)), tn=4 * GROUP)
            qm_col = D_MAIN // D_MEM
            sb_out = _sb_prompt(proj, kv, b_sb[j], bp, seq, emit_kv_t=(j == 0))
            main_p = sb_out[0]
            if j == 0:
                k_prompt, v_prompt = [a.reshape(bp, H_B, HD_B, seq).transpose(0, 3, 1, 2) for a in sb_out[1:]]
            bias_cols = jnp.pad(jnp.repeat(b_sb[j], lq), (0, page - H_B * lq)).reshape(1, page)
            main_s = _sb_sample(proj, kv, tp, bs, lq, ckt, cvt, page_table, bias_cols)
            w_main = wo[:D_MAIN].astype(BF16)

        mem_p = _memattn(proj, qm_col, 0, bp, seq, 512, mem_kv_p[l], 0, mem_kv_p[l], 1, BF16)
        mem_s = _memattn_small(proj, qm_col, tp, bs, lq, mem_len, cmkt, cmvt, l * bs)
        w_mem = wo[D_MAIN:].astype(BF16)
        x = _outproj(x, 0, main_p, w_main, mem_p, w_mem)
        x = _outproj(x, tp, main_s, w_main, mem_s, w_mem)
        if l < depth - 1:
            x = _ffn(x, g_ffn2[l], wu2, wd2, l)
        else:
            y_p = _ffn(x, g_ffn2[l], wu2, wd2, l, row0=0, nrows=tp)
            y_s = _ffn(x, g_ffn2[l], wu2, wd2, l, row0=tp, nrows=ts)

    mem_k_prompt = jnp.stack([m[:, :D_MEM] for m in mem_kv_p]).reshape(depth, bp, mem_len, H_MEM, HD_MEM)
    mem_v_prompt = jnp.stack([m[:, D_MEM:] for m in mem_kv_p]).reshape(depth, bp, mem_len, H_MEM, HD_MEM)
    return (y_p.reshape(bp, seq, d), y_s.reshape(bs, lq, d),
            k_prompt, v_prompt,
            kv[tp:, :D_MAIN].reshape(bs, lq, H_B, HD_B), kv[tp:, D_MAIN:].reshape(bs, lq, H_B, HD_B),
            mem_k_prompt, mem_v_prompt,
            jnp.stack(convs_p), c_p, jnp.stack(ns_p), jnp.stack(ms_p),
            jnp.stack(convs_s), c_s, jnp.stack(ns_s), jnp.stack(ms_s))
```

```python
import functools

import jax
import jax.numpy as jnp
from jax import lax
from jax.experimental import pallas as pl
from jax.experimental.pallas import tpu as pltpu

F32 = jnp.float32
BF16 = jnp.bfloat16

EPS = 1e-6
D_MODEL = 1024
H_MEM, HD_MEM = 4, 64
D_MEM = H_MEM * HD_MEM
D_MAIN = D_MODEL - D_MEM
H_A, HD_A = 4, 192
H_B, HD_B = 12, 64
HD_A_PAD = 256
D_A_PAD = H_A * HD_A_PAD
CONV_W = 4
D_FF = 2816
FF_CHUNK = 256
LANES = 128
GROUP = 256
ROW_TILE = 1024
FFN_ROW_TILE = 1024
VMEM_LIMIT = 56 * 1024 * 1024


def _cparams(sem, vmem=VMEM_LIMIT):
    return pltpu.CompilerParams(dimension_semantics=sem, vmem_limit_bytes=vmem)


def _rms(x, g):
    return x * lax.rsqrt(jnp.mean(x * x, axis=-1, keepdims=True) + EPS) * g


def _sigmoid(x):
    return 1.0 / (1.0 + jnp.exp(-x))


def _softplus(x):
    return jnp.maximum(x, 0.0) + jnp.log(1.0 + jnp.exp(-jnp.abs(x)))


def _head_group_matrix(n, width):
    r = lax.broadcasted_iota(jnp.int32, (n, n), 0) // width
    c = lax.broadcasted_iota(jnp.int32, (n, n), 1) // width
    return (r == c).astype(BF16)


def _ffn_kernel(x_ref, g_ref, wup_ref, wdn_ref, o_ref, h_sc, acc_sc):
    h_sc[...] = _rms(x_ref[...], g_ref[...]).astype(BF16)
    d_ff = wdn_ref.shape[1]
    for c in range(d_ff // FF_CHUNK):
        lo, hi = c * FF_CHUNK, (c + 1) * FF_CHUNK
        g = jnp.dot(h_sc[...], wup_ref[0, :, lo:hi], preferred_element_type=F32)
        u = jnp.dot(h_sc[...], wup_ref[0, :, d_ff + lo:d_ff + hi], preferred_element_type=F32)
        act = (g * _sigmoid(g) * u).astype(BF16)
        down = jnp.dot(act, wdn_ref[0, lo:hi, :], preferred_element_type=F32)
        if c == 0:
            acc_sc[...] = down
        else:
            acc_sc[...] += down
    o_ref[...] = x_ref[...] + 0.5 * acc_sc[...]


def _ffn(x, g, w_up, w_down, layer, row0=0, nrows=None):
    t, d = x.shape
    nrows = t if nrows is None else nrows
    d_ff = w_down.shape[1]
    tm = FFN_ROW_TILE
    rb0 = row0 // tm
    resident = pl.Buffered(1)
    return pl.pallas_call(
        _ffn_kernel,
        grid=(nrows // tm,),
        in_specs=[
            pl.BlockSpec((tm, d), lambda i: (rb0 + i, 0)),
            pl.BlockSpec((1, d), lambda i: (0, 0)),
            pl.BlockSpec((1, d, 2 * d_ff), lambda i: (layer, 0, 0), pipeline_mode=resident),
            pl.BlockSpec((1, d_ff, d), lambda i: (layer, 0, 0), pipeline_mode=resident),
        ],
        out_specs=pl.BlockSpec((tm, d), lambda i: (i, 0)),
        out_shape=jax.ShapeDtypeStruct((nrows, d), F32),
        scratch_shapes=[pltpu.VMEM((tm, d), BF16), pltpu.VMEM((tm, d), F32)],
        compiler_params=_cparams(("parallel",)),
        name="ffn",
    )(x, g.reshape(1, d), w_up, w_down)


def _proj_kernel(x_ref, g_ref, w_ref, gain_ref, o_ref, h_sc, *, normed_groups, head_width):
    h_sc[...] = _rms(x_ref[...], g_ref[...]).astype(BF16)
    for gi in range(w_ref.shape[1] // GROUP):
        sl = slice(gi * GROUP, (gi + 1) * GROUP)
        o_ref[:, sl] = jnp.dot(h_sc[...], w_ref[:, sl], preferred_element_type=F32)
    squares = [(o_ref[:, gi * GROUP:(gi + 1) * GROUP] ** 2).astype(BF16) for gi in normed_groups]
    sums = [jnp.dot(sq, _head_group_matrix(GROUP, head_width), preferred_element_type=F32) for sq in squares]
    for gi, ss in zip(normed_groups, sums):
        sl = slice(gi * GROUP, (gi + 1) * GROUP)
        o_ref[:, sl] = o_ref[:, sl] * lax.rsqrt(ss * (1.0 / head_width) + EPS) * gain_ref[:, sl]


def _proj(x, g, w, gain, normed_groups, tm=ROW_TILE, head_width=HD_B):
    t, d = x.shape
    n = w.shape[1]
    return pl.pallas_call(
        functools.partial(_proj_kernel, normed_groups=tuple(normed_groups), head_width=head_width),
        grid=(t // tm,),
        in_specs=[
            pl.BlockSpec((tm, d), lambda i: (i, 0)),
            pl.BlockSpec((1, d), lambda i: (0, 0)),
            pl.BlockSpec((d, n), lambda i: (0, 0), pipeline_mode=pl.Buffered(1)),
            pl.BlockSpec((1, n), lambda i: (0, 0)),
        ],
        out_specs=pl.BlockSpec((tm, n), lambda i: (i, 0)),
        out_shape=jax.ShapeDtypeStruct((t, n), F32),
        scratch_shapes=[pltpu.VMEM((tm, d), BF16)],
        compiler_params=_cparams(("parallel",)),
        name="proj",
    )(x, g.reshape(1, d), w, gain.reshape(1, n))


def _outproj_kernel(x_ref, a_ref, wa_ref, b_ref, wb_ref, o_ref):
    o_ref[...] = (x_ref[...]
                  + jnp.dot(a_ref[...].astype(BF16), wa_ref[...], preferred_element_type=F32)
                  + jnp.dot(b_ref[...].astype(BF16), wb_ref[...], preferred_element_type=F32))


def _outproj(x, row0, a, wa, b, wb):
    t, d = x.shape
    tm = ROW_TILE
    nrows, ka = a.shape
    kb = b.shape[1]
    rb0 = row0 // tm
    return pl.pallas_call(
        _outproj_kernel,
        grid=(nrows // tm,),
        in_specs=[
            pl.BlockSpec((tm, d), lambda i: (rb0 + i, 0)),
            pl.BlockSpec((tm, ka), lambda i: (i, 0)),
            pl.BlockSpec((ka, d), lambda i: (0, 0)),
            pl.BlockSpec((tm, kb), lambda i: (i, 0)),
            pl.BlockSpec((kb, d), lambda i: (0, 0)),
        ],
        out_specs=pl.BlockSpec((tm, d), lambda i: (rb0 + i, 0)),
        out_shape=jax.ShapeDtypeStruct((t, d), F32),
        input_output_aliases={0: 0},
        compiler_params=_cparams(("parallel",)),
        name="outproj",
    )(x, a, wa, b, wb)


def _memattn_kernel(q_ref, k_ref, v_ref, o_ref):
    q = q_ref[...]
    kb = k_ref[...].astype(BF16)
    vb = v_ref[...].astype(BF16)
    lane = lax.broadcasted_iota(jnp.int32, q.shape, 1) // HD_MEM
    scores = []
    for h in range(H_MEM):
        qh = jnp.where(lane == h, q, 0.0).astype(BF16)
        scores.append(lax.dot_general(qh, kb, (((1,), (1,)), ((), ())), preferred_element_type=F32)
                      * (HD_MEM ** -0.5))
    probs = []
    for s in scores:
        e = jnp.exp(s - jnp.max(s, axis=-1, keepdims=True))
        probs.append((e / jnp.sum(e, axis=-1, keepdims=True)).astype(BF16))
    acc = jnp.zeros(q.shape, F32)
    for h, p in enumerate(probs):
        acc = acc + jnp.where(lane == h, jnp.dot(p, vb, preferred_element_type=F32), 0.0)
    o_ref[...] = acc.astype(o_ref.dtype)


def _memattn(qarr, qcol, row0, nb, lq, tq, karr, kcol, varr, vcol, out_dtype):
    m = karr.shape[0] // nb
    nq = lq // tq
    rb0 = row0 // tq
    return pl.pallas_call(
        _memattn_kernel,
        grid=(nb, nq),
        in_specs=[
            pl.BlockSpec((tq, D_MEM), lambda b, i: (rb0 + b * nq + i, qcol)),
            pl.BlockSpec((m, D_MEM), lambda b, i: (b, kcol)),
            pl.BlockSpec((m, D_MEM), lambda b, i: (b, vcol)),
        ],
        out_specs=pl.BlockSpec((tq, D_MEM), lambda b, i: (b * nq + i, 0)),
        out_shape=jax.ShapeDtypeStruct((nb * lq, D_MEM), out_dtype),
        compiler_params=_cparams(("parallel", "parallel")),
        name="memattn",
    )(qarr, karr, varr)


def _memattn_small_kernel(q_ref, k_ref, v_ref, o_ref, *, nseq, lq):
    lane = lax.broadcasted_iota(jnp.int32, (lq, D_MEM), 1) // HD_MEM
    scores = []
    for s in range(nseq):
        q = q_ref[s * lq:(s + 1) * lq, :]
        qx = jnp.concatenate([jnp.where(lane == h, q, 0.0) for h in range(H_MEM)], axis=0).astype(BF16)
        kt = k_ref[s * D_MEM:(s + 1) * D_MEM, :].astype(BF16)
        scores.append(jnp.dot(qx, kt, preferred_element_type=F32) * (HD_MEM ** -0.5))
    probs = []
    for sc in scores:
        e = jnp.exp(sc - jnp.max(sc, axis=-1, keepdims=True))
        probs.append((e / jnp.sum(e, axis=-1, keepdims=True)).astype(BF16))
    for s, p in enumerate(probs):
        vt = v_ref[s * D_MEM:(s + 1) * D_MEM, :].astype(BF16)
        ox = lax.dot_general(p, vt, (((1,), (1,)), ((), ())), preferred_element_type=F32)
        out = jnp.zeros((lq, D_MEM), F32)
        for h in range(H_MEM):
            out = out + jnp.where(lane == h, ox[h * lq:(h + 1) * lq, :], 0.0)
        o_ref[s * lq:(s + 1) * lq, :] = out


def _memattn_small(qarr, qcol, row0, nb, lq, m, ktarr, vtarr, seq0, nseq=8):
    rb0 = row0 // (nseq * lq)
    kb0 = seq0 // nseq
    return pl.pallas_call(
        functools.partial(_memattn_small_kernel, nseq=nseq, lq=lq),
        grid=(nb // nseq,),
        in_specs=[
            pl.BlockSpec((nseq * lq, D_MEM), lambda i: (rb0 + i, qcol)),
            pl.BlockSpec((nseq * D_MEM, m), lambda i: (kb0 + i, 0)),
            pl.BlockSpec((nseq * D_MEM, m), lambda i: (kb0 + i, 0)),
        ],
        out_specs=pl.BlockSpec((nseq * lq, D_MEM), lambda i: (i, 0)),
        out_shape=jax.ShapeDtypeStruct((nb * lq, D_MEM), F32),
        compiler_params=_cparams(("parallel",)),
        name="memattn_small",
    )(qarr, ktarr, vtarr)


def _mlstm_kernel(qkvo_ref, gates_ref, convprev_ref, c0_ref, n0_ref, m0_ref,
                  convw_ref, convb_ref, bgate_ref, gh_ref, c_all_ref,
                  h_ref, c_out_ref, n_out_ref, m_out_ref,
                  ext_sc, c_sc, n_sc, m_sc, *, lb, lp, slot):
    ci = pl.program_id(1)
    last = lb - 1

    @pl.when(ci == 0)
    def _():
        ext_sc[0:8, :] = convprev_ref[0]
        c_sc[...] = jnp.zeros_like(c_sc)
        n_sc[...] = jnp.zeros_like(n_sc)
        for h in range(H_A):
            c_sc[h, 0:HD_A, 0:HD_A] = c0_ref[0, 0, h]
        n_sc[0:H_A, 0:HD_A] = n0_ref[0]
        m_sc[0:H_A, :] = m0_ref[0]

    def pad_rows(a):
        return a if lb == lp else jnp.concatenate([a, jnp.zeros((lp - lb, a.shape[1]), a.dtype)], axis=0)

    ext_sc[8:8 + lb, :] = qkvo_ref[:, 0:2 * D_A_PAD]

    row = lax.broadcasted_iota(jnp.int32, (lp, lp), 0)
    col = lax.broadcasted_iota(jnp.int32, (lp, lp), 1)
    causal = row >= col
    valid_row = lax.broadcasted_iota(jnp.int32, (lp, 1), 0) < lb

    gl = lax.broadcasted_iota(jnp.int32, (lp, GROUP), 1)
    gts = pad_rows(gates_ref[...]) + bgate_ref[...]
    gts = jnp.where((gl >= H_A) & (gl < 2 * H_A), -_softplus(-gts), gts)
    gts = jnp.where(valid_row & (gl < 2 * H_A), gts, 0.0)
    cum = jnp.dot(causal.astype(F32), gts, preferred_element_type=F32, precision=lax.Precision.HIGHEST)
    gts_t = gts.T
    cum_t = cum.T

    def conv_silu(c0):
        sl = slice(c0, c0 + HD_A_PAD)
        a = convb_ref[:, sl]
        for j in range(CONV_W):
            a = a + ext_sc[pl.ds(8 - (CONV_W - 1) + j, lb), sl] * convw_ref[j:j + 1, sl]
        return pad_rows(a * _sigmoid(a))

    heads = []
    for h in range(H_A):
        q = conv_silu(h * HD_A_PAD)
        k = conv_silu(D_A_PAD + h * HD_A_PAD) * (HD_A ** -0.5)
        vb = pad_rows(qkvo_ref[:, 2 * D_A_PAD + h * HD_A_PAD:2 * D_A_PAD + (h + 1) * HD_A_PAD]).astype(BF16)
        qb, kb = q.astype(BF16), k.astype(BF16)
        s = lax.dot_general(qb, kb, (((1,), (1,)), ((), ())), preferred_element_type=F32)
        cq = lax.dot_general(qb, c_sc[h].astype(BF16), (((1,), (1,)), ((), ())), preferred_element_type=F32)
        qn = jnp.sum(q * n_sc[h:h + 1, :], axis=-1, keepdims=True)
        heads.append((k, vb, s, cq, qn))

    for h, (k, vb, s, cq, qn) in enumerate(heads):
        hs = slice(h * HD_A_PAD, (h + 1) * HD_A_PAD)
        o_pre = qkvo_ref[:, 3 * D_A_PAD + h * HD_A_PAD:3 * D_A_PAD + (h + 1) * HD_A_PAD]

        ig_col = gts[:, h:h + 1]
        b_col = cum[:, H_A + h:H_A + h + 1]
        ig_row = gts_t[h:h + 1, :]
        b_row = cum_t[H_A + h:H_A + h + 1, :]
        m_prev = m_sc[h:h + 1, 0:1]
        c_prev = c_sc[h]
        n_prev = n_sc[h:h + 1, :]

        dlog = jnp.where(causal, b_col - b_row + ig_row, -jnp.inf)
        inter = b_col + m_prev
        m_t = jnp.maximum(inter, jnp.max(dlog, axis=-1, keepdims=True))
        w = jnp.exp(dlog - m_t) * s
        w_inter = jnp.exp(inter - m_t)
        num = jnp.dot(w.astype(BF16), vb, preferred_element_type=F32) + cq * w_inter
        den = jnp.sum(w, axis=-1, keepdims=True) + w_inter * qn
        den = jnp.maximum(jnp.abs(den), jnp.exp(-m_t))
        hh = num / den
        hn = hh * lax.rsqrt(jnp.sum(hh * hh, axis=-1, keepdims=True) * (1.0 / HD_A) + EPS) * gh_ref[:, hs]
        h_ref[:, hs] = (hn[0:lb, :] * _sigmoid(o_pre)).astype(h_ref.dtype)

        m_new = m_t[last:last + 1, :]
        b_last = b_col[last:last + 1, :]
        w_end = jnp.exp(b_last - b_col + ig_col - m_new)
        decay = jnp.exp(b_last + m_prev - m_new)
        kw = k * w_end
        c_sc[h] = decay * c_prev + lax.dot_general(vb, kw.astype(BF16), (((0,), (0,)), ((), ())),
                                                   preferred_element_type=F32)
        n_sc[h:h + 1, :] = decay * n_prev + jnp.sum(kw, axis=0, keepdims=True)
        m_sc[h:h + 1, :] = jnp.broadcast_to(m_new, (1, LANES))

    if lb == lp:
        ext_sc[0:8, :] = ext_sc[lp:lp + 8, :]

    @pl.when(ci == pl.num_programs(1) - 1)
    def _():
        for s in range(c_out_ref.shape[0]):
            if s != slot:
                c_out_ref[s] = jnp.zeros(c_out_ref.shape[1:], F32)
        for h in range(H_A):
            c_out_ref[slot, 0, h] = c_sc[h, 0:HD_A, 0:HD_A]
        n_out_ref[0] = n_sc[0:H_A, 0:HD_A]
        m_out_ref[0] = m_sc[0:H_A, :]


def _mlstm(proj, row0, nb, seq, lb, lp, convprev, c0, c0_slot, c_out_prev, layer, n_layers, n0, m0,
           convw, convb, bgate, gh, out_dtype):
    creates = c_out_prev is None
    nc = seq // lb
    rb0 = row0 // lb
    gate_col = (4 * D_A_PAD + D_MEM) // GROUP
    scratch = [
        pltpu.VMEM((lp + 8, 2 * D_A_PAD), F32),
        pltpu.VMEM((H_A, HD_A_PAD, HD_A_PAD), F32),
        pltpu.VMEM((8, HD_A_PAD), F32),
        pltpu.VMEM((8, LANES), F32),
    ]
    assert lb == lp or nc == 1, "a zero-padded tile must be the only chunk of its sequence"
    in_specs = [
        pl.BlockSpec((lb, 4 * D_A_PAD), lambda b, c: (rb0 + b * nc + c, 0)),
        pl.BlockSpec((lb, GROUP), lambda b, c: (rb0 + b * nc + c, gate_col)),
        pl.BlockSpec((1, 8, 2 * D_A_PAD), lambda b, c: (b, 0, 0)),
        pl.BlockSpec((1, 1, H_A, HD_A, HD_A), lambda b, c: (c0_slot, b, 0, 0, 0)),
        pl.BlockSpec((1, H_A, HD_A), lambda b, c: (b, 0, 0)),
        pl.BlockSpec((1, H_A, LANES), lambda b, c: (b, 0, 0)),
        pl.BlockSpec((CONV_W, 2 * D_A_PAD), lambda b, c: (0, 0)),
        pl.BlockSpec((1, 2 * D_A_PAD), lambda b, c: (0, 0)),
        pl.BlockSpec((1, GROUP), lambda b, c: (0, 0)),
        pl.BlockSpec((1, D_A_PAD), lambda b, c: (0, 0)),
    ]
    in_specs.append(pl.BlockSpec(memory_space=pl.ANY))
    args = [proj, proj, convprev, c0, n0, m0, convw, convb, bgate, gh, c0 if creates else c_out_prev]
    aliases = {} if creates else {len(args) - 1: 1}
    c_block = (n_layers, 0) if creates else (1, layer)
    return pl.pallas_call(
        functools.partial(_mlstm_kernel, lb=lb, lp=lp, slot=layer if creates else 0),
        grid=(nb, nc),
        in_specs=in_specs,
        out_specs=[
            pl.BlockSpec((lb, D_A_PAD), lambda b, c: (b * nc + c, 0)),
            pl.BlockSpec((c_block[0], 1, H_A, HD_A, HD_A), lambda b, c: (c_block[1], b, 0, 0, 0)),
            pl.BlockSpec((1, H_A, HD_A), lambda b, c: (b, 0, 0)),
            pl.BlockSpec((1, H_A, LANES), lambda b, c: (b, 0, 0)),
        ],
        out_shape=[
            jax.ShapeDtypeStruct((nb * seq, D_A_PAD), out_dtype),
            jax.ShapeDtypeStruct((n_layers, nb, H_A, HD_A, HD_A), F32),
            jax.ShapeDtypeStruct((nb, H_A, HD_A), F32),
            jax.ShapeDtypeStruct((nb, H_A, LANES), F32),
        ],
        scratch_shapes=scratch,
        input_output_aliases=aliases,
        compiler_params=_cparams(("parallel", "arbitrary")),
        name="mlstm",
    )(*args)


SB_BLOCK = 256
SB_QBLOCK = 1024
LOG2E = 1.4426950408889634


def _neg_later(n, keys_on_rows):
    r = lax.broadcasted_iota(jnp.int32, (n, n), 0)
    c = lax.broadcasted_iota(jnp.int32, (n, n), 1)
    later = (c > r) if keys_on_rows else (r > c)
    return jnp.where(later, -1.0, 0.0).astype(BF16)


def _softplus2(z):
    return jnp.where(z > 64.0, z, jnp.log(1.0 + jnp.exp2(z)) * LOG2E)


def _sb_weights(z_streams, masks, later_mat, carries, keys_on_rows=False, row0s=None):
    first = (lambda x: x[0:1, :]) if keys_on_rows else (lambda x: x[:, 0:1])
    row0s = [0] * len(masks) if row0s is None else row0s

    own, sp_first, inblock = [], [], []
    for zs in z_streams:
        o_s, f_s, i_s = [], [], []
        for z, m in zip(zs, masks):
            sp = _softplus2(z) if m is None else jnp.where(m, _softplus2(z), 0.0)
            o_s.append(z - sp)
            f_s.append(first(sp))
            spb = sp.astype(BF16)
            i_s.append(jnp.dot(later_mat, spb, preferred_element_type=F32) if keys_on_rows
                       else jnp.dot(spb, later_mat, preferred_element_type=F32))
        own.append(o_s)
        sp_first.append(f_s)
        inblock.append(i_s)
    weights, new_carries = [], []
    for o_s, f_s, i_s, carry in zip(own, sp_first, inblock, carries):
        ws = []
        for o, f, l, m, r0 in zip(o_s, f_s, i_s, masks, row0s):
            later = l + carry[r0:]
            a = jnp.exp2(o + later)
            ws.append(a if m is None else jnp.where(m, a, 0.0))
            new = first(later) - f
            carry = new if r0 == 0 else jnp.concatenate([carry[:r0], new], axis=0)
        weights.append(ws)
        new_carries.append(carry)
    return weights, tuple(new_carries)


def _sb_prompt_kernel(bias_ref, q_ref, k_ref, v_ref, o_ref, *rest, emit_kv_t):
    p = pl.program_id(1)
    i = pl.program_id(2)
    blk = SB_BLOCK
    nh = LANES // HD_B
    if emit_kv_t:
        kt_ref, vt_ref = rest[:2]
        rest = rest[2:]
    k_scs, v_scs = rest[:nh], rest[nh:]
    spare = [((hh + 1) % nh) * HD_B for hh in range(nh)]

    @pl.when(i == 0)
    def _():
        k = k_ref[...]
        v = v_ref[...]
        lane_id = lax.broadcasted_iota(jnp.int32, k.shape, 1)
        for hh in range(nh):
            ones = (lane_id == spare[hh]) | (lane_id == spare[hh] + 1)
            k_scs[hh][...] = jnp.where(lane_id // HD_B == hh, k, jnp.where(ones, 1.0, 0.0)).astype(BF16)
            v_scs[hh][...] = jnp.where(lane_id // HD_B == hh, v, 0.0).astype(BF16)
        if emit_kv_t:
            kt_ref[0] = k.T
            vt_ref[0] = v.T

    q = q_ref[...]
    tq = q.shape[0]
    per_q = tq // blk
    lane_id = lax.broadcasted_iota(jnp.int32, q.shape, 1)
    later_mat = _neg_later(blk, keys_on_rows=False)
    qh = []
    for hh in range(nh):
        b2 = jnp.full((1, LANES), bias_ref[p * nh + hh] * LOG2E, F32)
        b2_hi = b2.astype(BF16).astype(F32)
        extra = jnp.where(lane_id == spare[hh], b2_hi, jnp.where(lane_id == spare[hh] + 1, b2 - b2_hi, 0.0))
        qh.append(jnp.where(lane_id // HD_B == hh, q * (HD_B ** -0.5 * LOG2E), extra).astype(BF16))

    def tiles(js, masks, row0s, carries, acc):
        offs = [pl.multiple_of(j * blk, blk) for j in js]
        zs = [[lax.dot_general(qh[hh][r0:], k_scs[hh][pl.ds(off, blk), :], (((1,), (1,)), ((), ())),
                               preferred_element_type=F32) for off, r0 in zip(offs, row0s)] for hh in range(nh)]
        ws, carries = _sb_weights(zs, masks, later_mat, carries, row0s=row0s)
        for hh in range(nh):
            for off, a, r0 in zip(offs, ws[hh], row0s):
                upd = jnp.dot(a.astype(BF16), v_scs[hh][pl.ds(off, blk), :], preferred_element_type=F32)
                acc = acc + (upd if r0 == 0 else jnp.concatenate([jnp.zeros((r0, LANES), F32), upd], axis=0))
        return carries, acc

    rel = lax.broadcasted_iota(jnp.int32, (tq, blk), 0) - lax.broadcasted_iota(jnp.int32, (tq, blk), 1)
    first = i * per_q
    order = list(reversed(range(per_q)))
    zero_c = tuple(jnp.zeros((tq, 1), F32) for _ in range(nh))
    carries, acc = tiles([first + t for t in order], [rel[t * blk:] > t * blk for t in order],
                         [t * blk for t in order], zero_c, jnp.zeros(q.shape, F32))

    def body(t, state):
        j = first - 1 - t * per_q
        return tiles([j - u for u in range(per_q)], [None] * per_q, [0] * per_q, state[0], state[1])

    _, acc = lax.fori_loop(0, i, body, (carries, acc))
    o_ref[...] = acc.astype(o_ref.dtype)


def _sb_prompt(qarr, kvarr, bias, nb, seq, emit_kv_t):
    blk = SB_QBLOCK
    npair = D_MAIN // LANES
    ni = seq // blk
    out_specs = [pl.BlockSpec((blk, LANES), lambda b, p, i, s: (b * ni + i, p))]
    out_shape = [jax.ShapeDtypeStruct((nb * seq, D_MAIN), BF16)]
    if emit_kv_t:
        out_specs += [pl.BlockSpec((1, LANES, seq), lambda b, p, i, s: (b, p, 0))] * 2
        out_shape += [jax.ShapeDtypeStruct((nb, D_MAIN, seq), F32)] * 2
    grid_spec = pltpu.PrefetchScalarGridSpec(
        num_scalar_prefetch=1,
        grid=(nb, npair, ni),
        in_specs=[
            pl.BlockSpec((blk, LANES), lambda b, p, i, s: (b * ni + i, p)),
            pl.BlockSpec((seq, LANES), lambda b, p, i, s: (b, p)),
            pl.BlockSpec((seq, LANES), lambda b, p, i, s: (b, npair + p)),
        ],
        out_specs=out_specs,
        scratch_shapes=[pltpu.VMEM((seq, LANES), BF16)] * (2 * LANES // HD_B),
    )
    return pl.pallas_call(
        functools.partial(_sb_prompt_kernel, emit_kv_t=emit_kv_t),
        grid_spec=grid_spec,
        out_shape=out_shape,
        compiler_params=_cparams(("parallel", "parallel", "arbitrary")),
        name="sb_prompt",
    )(bias, qarr, kvarr, kvarr)


def _sb_sample_kernel(pt_ref, q_ref, knew_ref, vnew_ref, *refs, lq, page, npage):
    kt_refs, vt_refs = refs[:npage], refs[npage:2 * npage]
    bias_ref, o_ref, pad_sc = refs[2 * npage:]
    rows = H_B * lq
    later_mat = _neg_later(page, keys_on_rows=True)
    tn_dims = (((0,), (0,)), ((), ()))

    q = q_ref[...]
    lane = lax.broadcasted_iota(jnp.int32, q.shape, 1) // HD_B
    qx = jnp.concatenate([jnp.where(lane == h, q, 0.0) for h in range(H_B)]
                         + [jnp.zeros((page - rows, D_MAIN), F32)], axis=0)
    qxt = (qx * (HD_B ** -0.5 * LOG2E)).T.astype(BF16)
    bias = bias_ref[...] * LOG2E

    pad_sc[...] = jnp.zeros_like(pad_sc)
    pad_sc[0:lq, :] = knew_ref[...]
    knew = pad_sc[...].astype(BF16)
    pad_sc[0:lq, :] = vnew_ref[...]
    vnew = pad_sc[...].astype(BF16)
    kidx = lax.broadcasted_iota(jnp.int32, (page, page), 0)
    qidx = lax.broadcasted_iota(jnp.int32, (page, page), 1) % lq
    order = list(reversed(range(npage)))

    zs = [jnp.dot(knew, qxt, preferred_element_type=F32) + bias]
    zs += [lax.dot_general(kt_refs[g][0].astype(BF16), qxt, tn_dims, preferred_element_type=F32) + bias
           for g in order]
    masks = [kidx < qidx] + [None] * npage
    (ws,), _ = _sb_weights([zs], masks, later_mat, (jnp.zeros((1, page), F32),), keys_on_rows=True)

    acc = lax.dot_general(vnew, ws[0].astype(BF16), tn_dims, preferred_element_type=F32)
    for g, a in zip(order, ws[1:]):
        acc = acc + jnp.dot(vt_refs[g][0].astype(BF16), a.astype(BF16), preferred_element_type=F32)

    acc_t = acc.T
    olane = lax.broadcasted_iota(jnp.int32, (lq, D_MAIN), 1) // HD_B
    out = jnp.zeros((lq, D_MAIN), F32)
    for h in range(H_B):
        out = out + jnp.where(olane == h, acc_t[h * lq:(h + 1) * lq, :], 0.0)
    o_ref[...] = out


def _sb_sample(qarr, kvarr, row0, nb, lq, cache_kt, cache_vt, page_table, bias_cols):
    npage = page_table.shape[1]
    page = cache_kt.shape[2]
    rb0 = row0 // lq
    page_specs = [pl.BlockSpec((1, D_MAIN, page), lambda b, pt, g=g: (pt[b, g], 0, 0)) for g in range(npage)]
    grid_spec = pltpu.PrefetchScalarGridSpec(
        num_scalar_prefetch=1,
        grid=(nb,),
        in_specs=[
            pl.BlockSpec((lq, D_MAIN), lambda b, pt: (rb0 + b, 0)),
            pl.BlockSpec((lq, D_MAIN), lambda b, pt: (rb0 + b, 0)),
            pl.BlockSpec((lq, D_MAIN), lambda b, pt: (rb0 + b, 1)),
        ] + page_specs + page_specs + [pl.BlockSpec((1, page), lambda b, pt: (0, 0))],
        out_specs=pl.BlockSpec((lq, D_MAIN), lambda b, pt: (b, 0)),
        scratch_shapes=[pltpu.VMEM((page, D_MAIN), F32)],
    )
    return pl.pallas_call(
        functools.partial(_sb_sample_kernel, lq=lq, page=page, npage=npage),
        grid_spec=grid_spec,
        out_shape=jax.ShapeDtypeStruct((nb * lq, D_MAIN), F32),
        compiler_params=_cparams(("parallel",)),
        name="sb_sample",
    )(page_table, qarr, kvarr, kvarr, *([cache_kt] * npage), *([cache_vt] * npage), bias_cols)


def _pad_heads(w):
    lead = w.shape[:-1]
    w = w.reshape(lead + (H_A, HD_A))
    w = jnp.pad(w, [(0, 0)] * len(lead) + [(0, 0), (0, HD_A_PAD - HD_A)])
    return w.reshape(lead + (D_A_PAD,))


def _unpad_heads(w, nh):
    lead = w.shape[:-1]
    return w.reshape(lead + (nh, HD_A_PAD))[..., :HD_A].reshape(lead + (nh * HD_A,))


def kernel(x_prompt, x_sample, cache_k, cache_v, cache_mem_k, cache_mem_v, state_C, state_n, state_m, state_conv, page_table, mem_prompt, g_ffn1, w_ffn1_up, w_ffn1_down, g_mix, w_in_a, b_gate_a, conv_w_a, conv_b_a, g_h_a, g_kv, w_kv, g_k_b, w_in_b, g_q_b, b_sb, g_mem, w_mem_kv, g_q_mem, g_k_mem, w_out, g_ffn2, w_ffn2_up, w_ffn2_down):
    bp, seq, d = x_prompt.shape
    bs, lq, _ = x_sample.shape
    depth = g_ffn1.shape[0]
    n_a = w_in_a.shape[0]
    tp, ts = bp * seq, bs * lq
    mem_len = mem_prompt.shape[1]
    n_phys, page = cache_k.shape[0], cache_k.shape[1]

    x = jnp.concatenate([x_prompt.reshape(tp, d), x_sample.reshape(bs * lq, d)], axis=0)
    wu1, wd1 = w_ffn1_up.astype(BF16), w_ffn1_down.astype(BF16)
    wu2, wd2 = w_ffn2_up.astype(BF16), w_ffn2_down.astype(BF16)

    mem_rows = mem_prompt.reshape(bp * mem_len, d)
    mem_kv_p = []
    for l in range(depth):
        gain = jnp.concatenate([jnp.tile(g_k_mem[l], H_MEM), jnp.ones((D_MEM,), F32)])
        mem_kv_p.append(_proj(mem_rows, g_mem[l], w_mem_kv[l].astype(BF16), gain, (0,),
                              tm=bp * mem_len, head_width=HD_MEM))
    cmkt = cache_mem_k.transpose(0, 1, 3, 4, 2).reshape(depth * bs * D_MEM, mem_len)
    cmvt = cache_mem_v.transpose(0, 1, 3, 4, 2).reshape(depth * bs * D_MEM, mem_len)
    ckt = cache_k.transpose(0, 2, 3, 1).reshape(n_phys, D_MAIN, page)
    cvt = cache_v.transpose(0, 2, 3, 1).reshape(n_phys, D_MAIN, page)

    convs_p, convs_s, ns_p, ns_s, ms_p, ms_s = [], [], [], [], [], []
    c_p = c_s = None
    c0_p = jnp.zeros((1, bp, H_A, HD_A, HD_A), F32)
    kv = None
    for l in range(depth):
        if l == n_a:
            gain = jnp.concatenate([jnp.tile(g_k_b, H_B), jnp.ones((D_MAIN,), F32)])
            kv = _proj(x, g_kv, w_kv.astype(BF16), gain, tuple(range(D_MAIN // GROUP)))
        x = _ffn(x, g_ffn1[l], wu1, wd1, l)
        wo = w_out[l]
        if l < n_a:
            wi = w_in_a[l]
            q_w, k_w = wi[:, :D_MAIN], wi[:, D_MAIN:2 * D_MAIN]
            v_w, o_w = wi[:, 2 * D_MAIN:3 * D_MAIN], wi[:, 3 * D_MAIN:4 * D_MAIN]
            gate_w = wi[:, 4 * D_MAIN:4 * D_MAIN + 2 * H_A]
            qm_w = wi[:, 4 * D_MAIN + 2 * H_A:]
            w_a = jnp.concatenate([_pad_heads(q_w), _pad_heads(k_w), _pad_heads(v_w), _pad_heads(o_w), qm_w,
                                   jnp.pad(gate_w, ((0, 0), (0, GROUP - 2 * H_A)))], axis=1).astype(BF16)
            n_cols = w_a.shape[1]
            gain = jnp.concatenate([jnp.ones((4 * D_A_PAD,), F32), jnp.tile(g_q_mem[l], H_MEM),
                                    jnp.ones((GROUP,), F32)])
            proj = _proj(x, g_mix[l], w_a, gain, (4 * D_A_PAD // GROUP,), tm=ROW_TILE // 2, head_width=HD_MEM)
            qm_col = 4 * D_A_PAD // D_MEM

            convw = jnp.concatenate([_pad_heads(conv_w_a[l][:, :D_MAIN]), _pad_heads(conv_w_a[l][:, D_MAIN:])], axis=1)
            convb = jnp.concatenate([_pad_heads(conv_b_a[l][:D_MAIN]), _pad_heads(conv_b_a[l][D_MAIN:])]).reshape(1, -1)
            bgate = jnp.pad(b_gate_a[l], (0, GROUP - 2 * H_A)).reshape(1, GROUP)
            gh = _pad_heads(g_h_a[l].reshape(-1)).reshape(1, D_A_PAD)

            def conv_state(c):
                c = jnp.concatenate([_pad_heads(c[..., :D_MAIN]), _pad_heads(c[..., D_MAIN:])], axis=-1)
                return jnp.pad(c, ((0, 0), (8 - (CONV_W - 1), 0), (0, 0)))

            main_p, c_p, n_p, m_p = _mlstm(
                proj, 0, bp, seq, SB_BLOCK, SB_BLOCK,
                conv_state(jnp.zeros((bp, CONV_W - 1, 2 * D_MAIN), F32)), c0_p, 0, c_p, l, n_a,
                jnp.zeros((bp, H_A, HD_A), F32), jnp.zeros((bp, H_A, LANES), F32),
                convw, convb, bgate, gh, BF16)
            main_s, c_s, n_s, m_s = _mlstm(
                proj, tp, bs, lq, lq, LANES, conv_state(state_conv[l]), state_C, l, c_s, l, n_a, state_n[l],
                jnp.broadcast_to(state_m[l][..., None], (bs, H_A, LANES)), convw, convb, bgate, gh, F32)
            w_main = _pad_heads(wo[:D_MAIN].T).T.astype(BF16)

            tail = CONV_W - 1
            conv_rows_p = jnp.stack([proj[(b + 1) * seq - tail:(b + 1) * seq, :2 * D_A_PAD] for b in range(bp)])
            conv_rows_s = proj[tp:].reshape(bs, lq, -1)[:, lq - tail:, :2 * D_A_PAD]
            convs_p.append(_unpad_heads(conv_rows_p, 2 * H_A))
            convs_s.append(_unpad_heads(conv_rows_s, 2 * H_A))
            ns_p.append(n_p); ms_p.append(m_p[..., 0])
            ns_s.append(n_s); ms_s.append(m_s[..., 0])
        else:
            j = l - n_a
            gain = jnp.concatenate([jnp.tile(g_q_b[j], H_B), jnp.tile(g_q_mem[l], H_MEM)])
            proj = _proj(x, g_mix[l], w_in_b[j].astype(BF16), gain, tuple(range(D_MODEL // GROUP)))
            qm_col = D_MAIN // D_MEM
            sb_out = _sb_prompt(proj, kv, b_sb[j], bp, seq, emit_kv_t=(j == 0))
            main_p = sb_out[0]
            if j == 0:
                k_prompt, v_prompt = [a.reshape(bp, H_B, HD_B, seq).transpose(0, 3, 1, 2) for a in sb_out[1:]]
            bias_cols = jnp.pad(jnp.repeat(b_sb[j], lq), (0, page - H_B * lq)).reshape(1, page)
            main_s = _sb_sample(proj, kv, tp, bs, lq, ckt, cvt, page_table, bias_cols)
            w_main = wo[:D_MAIN].astype(BF16)

        mem_p = _memattn(proj, qm_col, 0, bp, seq, 512, mem_kv_p[l], 0, mem_kv_p[l], 1, BF16)
        mem_s = _memattn_small(proj, qm_col, tp, bs, lq, mem_len, cmkt, cmvt, l * bs)
        w_mem = wo[D_MAIN:].astype(BF16)
        x = _outproj(x, 0, main_p, w_main, mem_p, w_mem)
        x = _outproj(x, tp, main_s, w_main, mem_s, w_mem)
        if l < depth - 1:
            x = _ffn(x, g_ffn2[l], wu2, wd2, l)
        else:
            y_p = _ffn(x, g_ffn2[l], wu2, wd2, l, row0=0, nrows=tp)
            y_s = _ffn(x, g_ffn2[l], wu2, wd2, l, row0=tp, nrows=ts)

    mem_k_prompt = jnp.stack([m[:, :D_MEM] for m in mem_kv_p]).reshape(depth, bp, mem_len, H_MEM, HD_MEM)
    mem_v_prompt = jnp.stack([m[:, D_MEM:] for m in mem_kv_p]).reshape(depth, bp, mem_len, H_MEM, HD_MEM)
    return (y_p.reshape(bp, seq, d), y_s.reshape(bs, lq, d),
            k_prompt, v_prompt,
            kv[tp:, :D_MAIN].reshape(bs, lq, H_B, HD_B), kv[tp:, D_MAIN:].reshape(bs, lq, H_B, HD_B),
            mem_k_prompt, mem_v_prompt,
            jnp.stack(convs_p), c_p, jnp.stack(ns_p), jnp.stack(ms_p),
            jnp.stack(convs_s), c_s, jnp.stack(ns_s), jnp.stack(ms_s))
```

```python
import functools

import jax
import jax.numpy as jnp
from jax import lax
from jax.experimental import pallas as pl
from jax.experimental.pallas import tpu as pltpu

F32 = jnp.float32
BF16 = jnp.bfloat16

EPS = 1e-6
D_MODEL = 1024
H_MEM, HD_MEM = 4, 64
D_MEM = H_MEM * HD_MEM
D_MAIN = D_MODEL - D_MEM
H_A, HD_A = 4, 192
H_B, HD_B = 12, 64
HD_A_PAD = 256
D_A_PAD = H_A * HD_A_PAD
CONV_W = 4
D_FF = 2816
FF_CHUNK = 256
LANES = 128
GROUP = 256
ROW_TILE = 1024
FFN_ROW_TILE = 1024
VMEM_LIMIT = 56 * 1024 * 1024


def _cparams(sem, vmem=VMEM_LIMIT):
    return pltpu.CompilerParams(dimension_semantics=sem, vmem_limit_bytes=vmem)


def _rms(x, g):
    return x * lax.rsqrt(jnp.mean(x * x, axis=-1, keepdims=True) + EPS) * g


def _sigmoid(x):
    return 1.0 / (1.0 + jnp.exp(-x))


def _softplus(x):
    return jnp.maximum(x, 0.0) + jnp.log(1.0 + jnp.exp(-jnp.abs(x)))


def _head_group_matrix(n, width):
    r = lax.broadcasted_iota(jnp.int32, (n, n), 0) // width
    c = lax.broadcasted_iota(jnp.int32, (n, n), 1) // width
    return (r == c).astype(BF16)


def _ffn_kernel(*refs, mixed):
    if mixed:
        x_ref, a_ref, wa_ref, b_ref, wb_ref, g_ref, wup_ref, wdn_ref, o_ref, h_sc, acc_sc = refs
        o_ref[...] = (x_ref[...]
                      + jnp.dot(a_ref[...], wa_ref[...], preferred_element_type=F32)
                      + jnp.dot(b_ref[...], wb_ref[...], preferred_element_type=F32))
        x_ref = o_ref
    else:
        x_ref, g_ref, wup_ref, wdn_ref, o_ref, h_sc, acc_sc = refs
    h_sc[...] = _rms(x_ref[...], g_ref[...]).astype(BF16)
    d_ff = wdn_ref.shape[1]
    for c in range(d_ff // FF_CHUNK):
        lo, hi = c * FF_CHUNK, (c + 1) * FF_CHUNK
        g = jnp.dot(h_sc[...], wup_ref[0, :, lo:hi], preferred_element_type=F32)
        u = jnp.dot(h_sc[...], wup_ref[0, :, d_ff + lo:d_ff + hi], preferred_element_type=F32)
        act = (g * _sigmoid(g) * u).astype(BF16)
        down = jnp.dot(act, wdn_ref[0, lo:hi, :], preferred_element_type=F32)
        if c == 0:
            acc_sc[...] = down
        else:
            acc_sc[...] += down
    o_ref[...] = x_ref[...] + 0.5 * acc_sc[...]


def _ffn(x, g, w_up, w_down, layer, row0=0, nrows=None, mix=None, in_place=False):
    t, d = x.shape
    nrows = t if nrows is None else nrows
    d_ff = w_down.shape[1]
    tm = FFN_ROW_TILE
    rb0 = row0 // tm
    resident = pl.Buffered(1)
    in_specs = [pl.BlockSpec((tm, d), lambda i: (rb0 + i, 0))]
    args = [x]
    if mix is not None:
        a, wa, b, wb = mix
        ka, kb = a.shape[1], b.shape[1]
        in_specs += [
            pl.BlockSpec((tm, ka), lambda i: (i, 0)),
            pl.BlockSpec((ka, d), lambda i: (0, 0), pipeline_mode=resident),
            pl.BlockSpec((tm, kb), lambda i: (i, 0)),
            pl.BlockSpec((kb, d), lambda i: (0, 0), pipeline_mode=resident),
        ]
        args += [a, wa, b, wb]
    in_specs += [
        pl.BlockSpec((1, d), lambda i: (0, 0)),
        pl.BlockSpec((1, d, 2 * d_ff), lambda i: (layer, 0, 0), pipeline_mode=resident),
        pl.BlockSpec((1, d_ff, d), lambda i: (layer, 0, 0), pipeline_mode=resident),
    ]
    args += [g.reshape(1, d), w_up, w_down]
    out_rb0, out_rows = (rb0, t) if in_place else (0, nrows)
    return pl.pallas_call(
        functools.partial(_ffn_kernel, mixed=mix is not None),
        grid=(nrows // tm,),
        in_specs=in_specs,
        out_specs=pl.BlockSpec((tm, d), lambda i: (out_rb0 + i, 0)),
        out_shape=jax.ShapeDtypeStruct((out_rows, d), F32),
        scratch_shapes=[pltpu.VMEM((tm, d), BF16), pltpu.VMEM((tm, d), F32)],
        input_output_aliases={0: 0} if in_place else {},
        compiler_params=_cparams(("parallel",)),
        name="ffn",
    )(*args)


def _proj_kernel(x_ref, g_ref, w_ref, gain_ref, o_ref, h_sc, *, normed_groups, head_width):
    h_sc[...] = _rms(x_ref[...], g_ref[...]).astype(BF16)
    for gi in range(w_ref.shape[1] // GROUP):
        sl = slice(gi * GROUP, (gi + 1) * GROUP)
        o_ref[:, sl] = jnp.dot(h_sc[...], w_ref[:, sl], preferred_element_type=F32)
    squares = [(o_ref[:, gi * GROUP:(gi + 1) * GROUP] ** 2).astype(BF16) for gi in normed_groups]
    sums = [jnp.dot(sq, _head_group_matrix(GROUP, head_width), preferred_element_type=F32) for sq in squares]
    for gi, ss in zip(normed_groups, sums):
        sl = slice(gi * GROUP, (gi + 1) * GROUP)
        o_ref[:, sl] = o_ref[:, sl] * lax.rsqrt(ss * (1.0 / head_width) + EPS) * gain_ref[:, sl]


def _proj(x, g, w, gain, normed_groups, tm=ROW_TILE, head_width=HD_B):
    t, d = x.shape
    n = w.shape[1]
    return pl.pallas_call(
        functools.partial(_proj_kernel, normed_groups=tuple(normed_groups), head_width=head_width),
        grid=(t // tm,),
        in_specs=[
            pl.BlockSpec((tm, d), lambda i: (i, 0)),
            pl.BlockSpec((1, d), lambda i: (0, 0)),
            pl.BlockSpec((d, n), lambda i: (0, 0), pipeline_mode=pl.Buffered(1)),
            pl.BlockSpec((1, n), lambda i: (0, 0)),
        ],
        out_specs=pl.BlockSpec((tm, n), lambda i: (i, 0)),
        out_shape=jax.ShapeDtypeStruct((t, n), F32),
        scratch_shapes=[pltpu.VMEM((tm, d), BF16)],
        compiler_params=_cparams(("parallel",)),
        name="proj",
    )(x, g.reshape(1, d), w, gain.reshape(1, n))


def _memattn_kernel(q_ref, k_ref, v_ref, o_ref):
    q = q_ref[...]
    kb = k_ref[...].astype(BF16)
    vb = v_ref[...].astype(BF16)
    lane = lax.broadcasted_iota(jnp.int32, q.shape, 1) // HD_MEM
    scores = []
    for h in range(H_MEM):
        qh = jnp.where(lane == h, q, 0.0).astype(BF16)
        scores.append(lax.dot_general(qh, kb, (((1,), (1,)), ((), ())), preferred_element_type=F32)
                      * (HD_MEM ** -0.5))
    probs = []
    for s in scores:
        e = jnp.exp(s - jnp.max(s, axis=-1, keepdims=True))
        probs.append((e / jnp.sum(e, axis=-1, keepdims=True)).astype(BF16))
    acc = jnp.zeros(q.shape, F32)
    for h, p in enumerate(probs):
        acc = acc + jnp.where(lane == h, jnp.dot(p, vb, preferred_element_type=F32), 0.0)
    o_ref[...] = acc.astype(o_ref.dtype)


def _memattn(qarr, qcol, row0, nb, lq, tq, karr, kcol, varr, vcol, out_dtype):
    m = karr.shape[0] // nb
    nq = lq // tq
    rb0 = row0 // tq
    return pl.pallas_call(
        _memattn_kernel,
        grid=(nb, nq),
        in_specs=[
            pl.BlockSpec((tq, D_MEM), lambda b, i: (rb0 + b * nq + i, qcol)),
            pl.BlockSpec((m, D_MEM), lambda b, i: (b, kcol)),
            pl.BlockSpec((m, D_MEM), lambda b, i: (b, vcol)),
        ],
        out_specs=pl.BlockSpec((tq, D_MEM), lambda b, i: (b * nq + i, 0)),
        out_shape=jax.ShapeDtypeStruct((nb * lq, D_MEM), out_dtype),
        compiler_params=_cparams(("parallel", "parallel")),
        name="memattn",
    )(qarr, karr, varr)


def _memattn_small_kernel(q_ref, k_ref, v_ref, o_ref, *, nseq, lq):
    lane = lax.broadcasted_iota(jnp.int32, (lq, D_MEM), 1) // HD_MEM
    scores = []
    for s in range(nseq):
        q = q_ref[s * lq:(s + 1) * lq, :]
        qx = jnp.concatenate([jnp.where(lane == h, q, 0.0) for h in range(H_MEM)], axis=0).astype(BF16)
        kt = k_ref[s * D_MEM:(s + 1) * D_MEM, :].astype(BF16)
        scores.append(jnp.dot(qx, kt, preferred_element_type=F32) * (HD_MEM ** -0.5))
    probs = []
    for sc in scores:
        e = jnp.exp(sc - jnp.max(sc, axis=-1, keepdims=True))
        probs.append((e / jnp.sum(e, axis=-1, keepdims=True)).astype(BF16))
    for s, p in enumerate(probs):
        vt = v_ref[s * D_MEM:(s + 1) * D_MEM, :].astype(BF16)
        ox = lax.dot_general(p, vt, (((1,), (1,)), ((), ())), preferred_element_type=F32)
        out = jnp.zeros((lq, D_MEM), F32)
        for h in range(H_MEM):
            out = out + jnp.where(lane == h, ox[h * lq:(h + 1) * lq, :], 0.0)
        o_ref[s * lq:(s + 1) * lq, :] = out


def _memattn_small(qarr, qcol, row0, nb, lq, m, ktarr, vtarr, seq0, nseq=8):
    rb0 = row0 // (nseq * lq)
    kb0 = seq0 // nseq
    return pl.pallas_call(
        functools.partial(_memattn_small_kernel, nseq=nseq, lq=lq),
        grid=(nb // nseq,),
        in_specs=[
            pl.BlockSpec((nseq * lq, D_MEM), lambda i: (rb0 + i, qcol)),
            pl.BlockSpec((nseq * D_MEM, m), lambda i: (kb0 + i, 0)),
            pl.BlockSpec((nseq * D_MEM, m), lambda i: (kb0 + i, 0)),
        ],
        out_specs=pl.BlockSpec((nseq * lq, D_MEM), lambda i: (i, 0)),
        out_shape=jax.ShapeDtypeStruct((nb * lq, D_MEM), F32),
        compiler_params=_cparams(("parallel",)),
        name="memattn_small",
    )(qarr, ktarr, vtarr)


def _mlstm_kernel(qkvo_ref, gates_ref, convprev_ref, c0_ref, n0_ref, m0_ref,
                  convw_ref, convb_ref, bgate_ref, gh_ref, c_all_ref,
                  h_ref, c_out_ref, n_out_ref, m_out_ref,
                  ext_sc, c_sc, n_sc, m_sc, *, lb, lp, slot):
    ci = pl.program_id(1)
    last = lb - 1

    @pl.when(ci == 0)
    def _():
        ext_sc[0:8, :] = convprev_ref[0]
        c_sc[...] = jnp.zeros_like(c_sc)
        n_sc[...] = jnp.zeros_like(n_sc)
        for h in range(H_A):
            c_sc[h, 0:HD_A, 0:HD_A] = c0_ref[0, 0, h]
        n_sc[0:H_A, 0:HD_A] = n0_ref[0]
        m_sc[0:H_A, :] = m0_ref[0]

    def pad_rows(a):
        return a if lb == lp else jnp.concatenate([a, jnp.zeros((lp - lb, a.shape[1]), a.dtype)], axis=0)

    ext_sc[8:8 + lb, :] = qkvo_ref[:, 0:2 * D_A_PAD]

    row = lax.broadcasted_iota(jnp.int32, (lp, lp), 0)
    col = lax.broadcasted_iota(jnp.int32, (lp, lp), 1)
    causal = row >= col
    valid_row = lax.broadcasted_iota(jnp.int32, (lp, 1), 0) < lb

    gl = lax.broadcasted_iota(jnp.int32, (lp, GROUP), 1)
    gts = pad_rows(gates_ref[...]) + bgate_ref[...]
    gts = jnp.where((gl >= H_A) & (gl < 2 * H_A), -_softplus(-gts), gts)
    gts = jnp.where(valid_row & (gl < 2 * H_A), gts, 0.0)
    cum = jnp.dot(causal.astype(F32), gts, preferred_element_type=F32, precision=lax.Precision.HIGHEST)
    gts_t = gts.T
    cum_t = cum.T

    def conv_silu(c0):
        sl = slice(c0, c0 + HD_A_PAD)
        a = convb_ref[:, sl]
        for j in range(CONV_W):
            a = a + ext_sc[pl.ds(8 - (CONV_W - 1) + j, lb), sl] * convw_ref[j:j + 1, sl]
        return pad_rows(a * _sigmoid(a))

    heads = []
    for h in range(H_A):
        q = conv_silu(h * HD_A_PAD)
        k = conv_silu(D_A_PAD + h * HD_A_PAD) * (HD_A ** -0.5)
        vb = pad_rows(qkvo_ref[:, 2 * D_A_PAD + h * HD_A_PAD:2 * D_A_PAD + (h + 1) * HD_A_PAD]).astype(BF16)
        qb, kb = q.astype(BF16), k.astype(BF16)
        s = lax.dot_general(qb, kb, (((1,), (1,)), ((), ())), preferred_element_type=F32)
        cq = lax.dot_general(qb, c_sc[h].astype(BF16), (((1,), (1,)), ((), ())), preferred_element_type=F32)
        qn = jnp.sum(q * n_sc[h:h + 1, :], axis=-1, keepdims=True)
        heads.append((k, vb, s, cq, qn))

    for h, (k, vb, s, cq, qn) in enumerate(heads):
        hs = slice(h * HD_A_PAD, (h + 1) * HD_A_PAD)
        o_pre = qkvo_ref[:, 3 * D_A_PAD + h * HD_A_PAD:3 * D_A_PAD + (h + 1) * HD_A_PAD]

        ig_col = gts[:, h:h + 1]
        b_col = cum[:, H_A + h:H_A + h + 1]
        ig_row = gts_t[h:h + 1, :]
        b_row = cum_t[H_A + h:H_A + h + 1, :]
        m_prev = m_sc[h:h + 1, 0:1]
        c_prev = c_sc[h]
        n_prev = n_sc[h:h + 1, :]

        dlog = jnp.where(causal, b_col - b_row + ig_row, -jnp.inf)
        inter = b_col + m_prev
        m_t = jnp.maximum(inter, jnp.max(dlog, axis=-1, keepdims=True))
        w = jnp.exp(dlog - m_t) * s
        w_inter = jnp.exp(inter - m_t)
        num = jnp.dot(w.astype(BF16), vb, preferred_element_type=F32) + cq * w_inter
        den = jnp.sum(w, axis=-1, keepdims=True) + w_inter * qn
        den = jnp.maximum(jnp.abs(den), jnp.exp(-m_t))
        hh = num / den
        hn = hh * lax.rsqrt(jnp.sum(hh * hh, axis=-1, keepdims=True) * (1.0 / HD_A) + EPS) * gh_ref[:, hs]
        h_ref[:, hs] = (hn[0:lb, :] * _sigmoid(o_pre)).astype(h_ref.dtype)

        m_new = m_t[last:last + 1, :]
        b_last = b_col[last:last + 1, :]
        w_end = jnp.exp(b_last - b_col + ig_col - m_new)
        decay = jnp.exp(b_last + m_prev - m_new)
        kw = k * w_end
        c_sc[h] = decay * c_prev + lax.dot_general(vb, kw.astype(BF16), (((0,), (0,)), ((), ())),
                                                   preferred_element_type=F32)
        n_sc[h:h + 1, :] = decay * n_prev + jnp.sum(kw, axis=0, keepdims=True)
        m_sc[h:h + 1, :] = jnp.broadcast_to(m_new, (1, LANES))

    if lb == lp:
        ext_sc[0:8, :] = ext_sc[lp:lp + 8, :]

    @pl.when(ci == pl.num_programs(1) - 1)
    def _():
        for s in range(c_out_ref.shape[0]):
            if s != slot:
                c_out_ref[s] = jnp.zeros(c_out_ref.shape[1:], F32)
        for h in range(H_A):
            c_out_ref[slot, 0, h] = c_sc[h, 0:HD_A, 0:HD_A]
        n_out_ref[0] = n_sc[0:H_A, 0:HD_A]
        m_out_ref[0] = m_sc[0:H_A, :]


def _mlstm(proj, row0, nb, seq, lb, lp, convprev, c0, c0_slot, c_out_prev, layer, n_layers, n0, m0,
           convw, convb, bgate, gh, out_dtype):
    creates = c_out_prev is None
    nc = seq // lb
    rb0 = row0 // lb
    gate_col = (4 * D_A_PAD + D_MEM) // GROUP
    scratch = [
        pltpu.VMEM((lp + 8, 2 * D_A_PAD), F32),
        pltpu.VMEM((H_A, HD_A_PAD, HD_A_PAD), F32),
        pltpu.VMEM((8, HD_A_PAD), F32),
        pltpu.VMEM((8, LANES), F32),
    ]
    assert lb == lp or nc == 1, "a zero-padded tile must be the only chunk of its sequence"
    in_specs = [
        pl.BlockSpec((lb, 4 * D_A_PAD), lambda b, c: (rb0 + b * nc + c, 0)),
        pl.BlockSpec((lb, GROUP), lambda b, c: (rb0 + b * nc + c, gate_col)),
        pl.BlockSpec((1, 8, 2 * D_A_PAD), lambda b, c: (b, 0, 0)),
        pl.BlockSpec((1, 1, H_A, HD_A, HD_A), lambda b, c: (c0_slot, b, 0, 0, 0)),
        pl.BlockSpec((1, H_A, HD_A), lambda b, c: (b, 0, 0)),
        pl.BlockSpec((1, H_A, LANES), lambda b, c: (b, 0, 0)),
        pl.BlockSpec((CONV_W, 2 * D_A_PAD), lambda b, c: (0, 0)),
        pl.BlockSpec((1, 2 * D_A_PAD), lambda b, c: (0, 0)),
        pl.BlockSpec((1, GROUP), lambda b, c: (0, 0)),
        pl.BlockSpec((1, D_A_PAD), lambda b, c: (0, 0)),
    ]
    in_specs.append(pl.BlockSpec(memory_space=pl.ANY))
    args = [proj, proj, convprev, c0, n0, m0, convw, convb, bgate, gh, c0 if creates else c_out_prev]
    aliases = {} if creates else {len(args) - 1: 1}
    c_block = (n_layers, 0) if creates else (1, layer)
    return pl.pallas_call(
        functools.partial(_mlstm_kernel, lb=lb, lp=lp, slot=layer if creates else 0),
        grid=(nb, nc),
        in_specs=in_specs,
        out_specs=[
            pl.BlockSpec((lb, D_A_PAD), lambda b, c: (b * nc + c, 0)),
            pl.BlockSpec((c_block[0], 1, H_A, HD_A, HD_A), lambda b, c: (c_block[1], b, 0, 0, 0)),
            pl.BlockSpec((1, H_A, HD_A), lambda b, c: (b, 0, 0)),
            pl.BlockSpec((1, H_A, LANES), lambda b, c: (b, 0, 0)),
        ],
        out_shape=[
            jax.ShapeDtypeStruct((nb * seq, D_A_PAD), out_dtype),
            jax.ShapeDtypeStruct((n_layers, nb, H_A, HD_A, HD_A), F32),
            jax.ShapeDtypeStruct((nb, H_A, HD_A), F32),
            jax.ShapeDtypeStruct((nb, H_A, LANES), F32),
        ],
        scratch_shapes=scratch,
        input_output_aliases=aliases,
        compiler_params=_cparams(("parallel", "arbitrary")),
        name="mlstm",
    )(*args)


SB_BLOCK = 256
SB_QBLOCK = 1024
LOG2E = 1.4426950408889634


def _neg_later(n, keys_on_rows):
    r = lax.broadcasted_iota(jnp.int32, (n, n), 0)
    c = lax.broadcasted_iota(jnp.int32, (n, n), 1)
    later = (c > r) if keys_on_rows else (r > c)
    return jnp.where(later, -1.0, 0.0).astype(BF16)


def _softplus2(z):
    return jnp.where(z > 64.0, z, jnp.log(1.0 + jnp.exp2(z)) * LOG2E)


def _sb_weights(z_streams, masks, later_mat, carries, keys_on_rows=False, row0s=None):
    first = (lambda x: x[0:1, :]) if keys_on_rows else (lambda x: x[:, 0:1])
    row0s = [0] * len(masks) if row0s is None else row0s

    own, sp_first, inblock = [], [], []
    for zs in z_streams:
        o_s, f_s, i_s = [], [], []
        for z, m in zip(zs, masks):
            sp = _softplus2(z) if m is None else jnp.where(m, _softplus2(z), 0.0)
            o_s.append(z - sp)
            f_s.append(first(sp))
            spb = sp.astype(BF16)
            i_s.append(jnp.dot(later_mat, spb, preferred_element_type=F32) if keys_on_rows
                       else jnp.dot(spb, later_mat, preferred_element_type=F32))
        own.append(o_s)
        sp_first.append(f_s)
        inblock.append(i_s)
    weights, new_carries = [], []
    for o_s, f_s, i_s, carry in zip(own, sp_first, inblock, carries):
        ws = []
        for o, f, l, m, r0 in zip(o_s, f_s, i_s, masks, row0s):
            later = l + carry[r0:]
            a = jnp.exp2(o + later)
            ws.append(a if m is None else jnp.where(m, a, 0.0))
            new = first(later) - f
            carry = new if r0 == 0 else jnp.concatenate([carry[:r0], new], axis=0)
        weights.append(ws)
        new_carries.append(carry)
    return weights, tuple(new_carries)


def _sb_prompt_kernel(bias_ref, q_ref, k_ref, v_ref, o_ref, *rest, emit_kv_t):
    p = pl.program_id(1)
    i = pl.program_id(2)
    blk = SB_BLOCK
    nh = LANES // HD_B
    if emit_kv_t:
        kt_ref, vt_ref = rest[:2]
        rest = rest[2:]
    k_scs, v_scs = rest[:nh], rest[nh:]
    spare = [((hh + 1) % nh) * HD_B for hh in range(nh)]

    @pl.when(i == 0)
    def _():
        k = k_ref[...]
        v = v_ref[...]
        lane_id = lax.broadcasted_iota(jnp.int32, k.shape, 1)
        for hh in range(nh):
            ones = (lane_id == spare[hh]) | (lane_id == spare[hh] + 1)
            k_scs[hh][...] = jnp.where(lane_id // HD_B == hh, k, jnp.where(ones, 1.0, 0.0)).astype(BF16)
            v_scs[hh][...] = jnp.where(lane_id // HD_B == hh, v, 0.0).astype(BF16)
        if emit_kv_t:
            kt_ref[0] = k.T
            vt_ref[0] = v.T

    q = q_ref[...]
    tq = q.shape[0]
    per_q = tq // blk
    lane_id = lax.broadcasted_iota(jnp.int32, q.shape, 1)
    later_mat = _neg_later(blk, keys_on_rows=False)
    qh = []
    for hh in range(nh):
        b2 = jnp.full((1, LANES), bias_ref[p * nh + hh] * LOG2E, F32)
        b2_hi = b2.astype(BF16).astype(F32)
        extra = jnp.where(lane_id == spare[hh], b2_hi, jnp.where(lane_id == spare[hh] + 1, b2 - b2_hi, 0.0))
        qh.append(jnp.where(lane_id // HD_B == hh, q * (HD_B ** -0.5 * LOG2E), extra).astype(BF16))

    def tiles(js, masks, row0s, carries, acc):
        offs = [pl.multiple_of(j * blk, blk) for j in js]
        zs = [[lax.dot_general(qh[hh][r0:], k_scs[hh][pl.ds(off, blk), :], (((1,), (1,)), ((), ())),
                               preferred_element_type=F32) for off, r0 in zip(offs, row0s)] for hh in range(nh)]
        ws, carries = _sb_weights(zs, masks, later_mat, carries, row0s=row0s)
        for hh in range(nh):
            for off, a, r0 in zip(offs, ws[hh], row0s):
                upd = jnp.dot(a.astype(BF16), v_scs[hh][pl.ds(off, blk), :], preferred_element_type=F32)
                acc = acc + (upd if r0 == 0 else jnp.concatenate([jnp.zeros((r0, LANES), F32), upd], axis=0))
        return carries, acc

    rel = lax.broadcasted_iota(jnp.int32, (tq, blk), 0) - lax.broadcasted_iota(jnp.int32, (tq, blk), 1)
    first = i * per_q
    order = list(reversed(range(per_q)))
    zero_c = tuple(jnp.zeros((tq, 1), F32) for _ in range(nh))
    carries, acc = tiles([first + t for t in order], [rel[t * blk:] > t * blk for t in order],
                         [t * blk for t in order], zero_c, jnp.zeros(q.shape, F32))

    def body(t, state):
        j = first - 1 - t * per_q
        return tiles([j - u for u in range(per_q)], [None] * per_q, [0] * per_q, state[0], state[1])

    _, acc = lax.fori_loop(0, i, body, (carries, acc))
    o_ref[...] = acc.astype(o_ref.dtype)


def _sb_prompt(qarr, kvarr, bias, nb, seq, emit_kv_t):
    blk = SB_QBLOCK
    npair = D_MAIN // LANES
    ni = seq // blk
    out_specs = [pl.BlockSpec((blk, LANES), lambda b, p, i, s: (b * ni + i, p))]
    out_shape = [jax.ShapeDtypeStruct((nb * seq, D_MAIN), BF16)]
    if emit_kv_t:
        out_specs += [pl.BlockSpec((1, LANES, seq), lambda b, p, i, s: (b, p, 0))] * 2
        out_shape += [jax.ShapeDtypeStruct((nb, D_MAIN, seq), F32)] * 2
    grid_spec = pltpu.PrefetchScalarGridSpec(
        num_scalar_prefetch=1,
        grid=(nb, npair, ni),
        in_specs=[
            pl.BlockSpec((blk, LANES), lambda b, p, i, s: (b * ni + i, p)),
            pl.BlockSpec((seq, LANES), lambda b, p, i, s: (b, p)),
            pl.BlockSpec((seq, LANES), lambda b, p, i, s: (b, npair + p)),
        ],
        out_specs=out_specs,
        scratch_shapes=[pltpu.VMEM((seq, LANES), BF16)] * (2 * LANES // HD_B),
    )
    return pl.pallas_call(
        functools.partial(_sb_prompt_kernel, emit_kv_t=emit_kv_t),
        grid_spec=grid_spec,
        out_shape=out_shape,
        compiler_params=_cparams(("parallel", "parallel", "arbitrary")),
        name="sb_prompt",
    )(bias, qarr, kvarr, kvarr)


def _sb_sample_kernel(pt_ref, q_ref, knew_ref, vnew_ref, *refs, lq, page, npage):
    kt_refs, vt_refs = refs[:npage], refs[npage:2 * npage]
    bias_ref, o_ref, pad_sc = refs[2 * npage:]
    rows = H_B * lq
    later_mat = _neg_later(page, keys_on_rows=True)
    tn_dims = (((0,), (0,)), ((), ()))

    q = q_ref[...]
    lane = lax.broadcasted_iota(jnp.int32, q.shape, 1) // HD_B
    qx = jnp.concatenate([jnp.where(lane == h, q, 0.0) for h in range(H_B)]
                         + [jnp.zeros((page - rows, D_MAIN), F32)], axis=0)
    qxt = (qx * (HD_B ** -0.5 * LOG2E)).T.astype(BF16)
    bias = bias_ref[...] * LOG2E

    pad_sc[...] = jnp.zeros_like(pad_sc)
    pad_sc[0:lq, :] = knew_ref[...]
    knew = pad_sc[...].astype(BF16)
    pad_sc[0:lq, :] = vnew_ref[...]
    vnew = pad_sc[...].astype(BF16)
    kidx = lax.broadcasted_iota(jnp.int32, (page, page), 0)
    qidx = lax.broadcasted_iota(jnp.int32, (page, page), 1) % lq
    order = list(reversed(range(npage)))

    zs = [jnp.dot(knew, qxt, preferred_element_type=F32) + bias]
    zs += [lax.dot_general(kt_refs[g][0].astype(BF16), qxt, tn_dims, preferred_element_type=F32) + bias
           for g in order]
    masks = [kidx < qidx] + [None] * npage
    (ws,), _ = _sb_weights([zs], masks, later_mat, (jnp.zeros((1, page), F32),), keys_on_rows=True)

    acc = lax.dot_general(vnew, ws[0].astype(BF16), tn_dims, preferred_element_type=F32)
    for g, a in zip(order, ws[1:]):
        acc = acc + jnp.dot(vt_refs[g][0].astype(BF16), a.astype(BF16), preferred_element_type=F32)

    acc_t = acc.T
    olane = lax.broadcasted_iota(jnp.int32, (lq, D_MAIN), 1) // HD_B
    out = jnp.zeros((lq, D_MAIN), F32)
    for h in range(H_B):
        out = out + jnp.where(olane == h, acc_t[h * lq:(h + 1) * lq, :], 0.0)
    o_ref[...] = out


def _sb_sample(qarr, kvarr, row0, nb, lq, cache_kt, cache_vt, page_table, bias_cols):
    npage = page_table.shape[1]
    page = cache_kt.shape[2]
    rb0 = row0 // lq
    page_specs = [pl.BlockSpec((1, D_MAIN, page), lambda b, pt, g=g: (pt[b, g], 0, 0)) for g in range(npage)]
    grid_spec = pltpu.PrefetchScalarGridSpec(
        num_scalar_prefetch=1,
        grid=(nb,),
        in_specs=[
            pl.BlockSpec((lq, D_MAIN), lambda b, pt: (rb0 + b, 0)),
            pl.BlockSpec((lq, D_MAIN), lambda b, pt: (rb0 + b, 0)),
            pl.BlockSpec((lq, D_MAIN), lambda b, pt: (rb0 + b, 1)),
        ] + page_specs + page_specs + [pl.BlockSpec((1, page), lambda b, pt: (0, 0))],
        out_specs=pl.BlockSpec((lq, D_MAIN), lambda b, pt: (b, 0)),
        scratch_shapes=[pltpu.VMEM((page, D_MAIN), F32)],
    )
    return pl.pallas_call(
        functools.partial(_sb_sample_kernel, lq=lq, page=page, npage=npage),
        grid_spec=grid_spec,
        out_shape=jax.ShapeDtypeStruct((nb * lq, D_MAIN), F32),
        compiler_params=_cparams(("parallel",)),
        name="sb_sample",
    )(page_table, qarr, kvarr, kvarr, *([cache_kt] * npage), *([cache_vt] * npage), bias_cols)


def _pad_heads(w):
    lead = w.shape[:-1]
    w = w.reshape(lead + (H_A, HD_A))
    w = jnp.pad(w, [(0, 0)] * len(lead) + [(0, 0), (0, HD_A_PAD - HD_A)])
    return w.reshape(lead + (D_A_PAD,))


def _unpad_heads(w, nh):
    lead = w.shape[:-1]
    return w.reshape(lead + (nh, HD_A_PAD))[..., :HD_A].reshape(lead + (nh * HD_A,))


def kernel(x_prompt, x_sample, cache_k, cache_v, cache_mem_k, cache_mem_v, state_C, state_n, state_m, state_conv, page_table, mem_prompt, g_ffn1, w_ffn1_up, w_ffn1_down, g_mix, w_in_a, b_gate_a, conv_w_a, conv_b_a, g_h_a, g_kv, w_kv, g_k_b, w_in_b, g_q_b, b_sb, g_mem, w_mem_kv, g_q_mem, g_k_mem, w_out, g_ffn2, w_ffn2_up, w_ffn2_down):
    bp, seq, d = x_prompt.shape
    bs, lq, _ = x_sample.shape
    depth = g_ffn1.shape[0]
    n_a = w_in_a.shape[0]
    tp, ts = bp * seq, bs * lq
    mem_len = mem_prompt.shape[1]
    n_phys, page = cache_k.shape[0], cache_k.shape[1]

    x = jnp.concatenate([x_prompt.reshape(tp, d), x_sample.reshape(bs * lq, d)], axis=0)
    wu1, wd1 = w_ffn1_up.astype(BF16), w_ffn1_down.astype(BF16)
    wu2, wd2 = w_ffn2_up.astype(BF16), w_ffn2_down.astype(BF16)

    mem_rows = mem_prompt.reshape(bp * mem_len, d)
    mem_kv_p = []
    for l in range(depth):
        gain = jnp.concatenate([jnp.tile(g_k_mem[l], H_MEM), jnp.ones((D_MEM,), F32)])
        mem_kv_p.append(_proj(mem_rows, g_mem[l], w_mem_kv[l].astype(BF16), gain, (0,),
                              tm=bp * mem_len, head_width=HD_MEM))
    cmkt = cache_mem_k.transpose(0, 1, 3, 4, 2).reshape(depth * bs * D_MEM, mem_len)
    cmvt = cache_mem_v.transpose(0, 1, 3, 4, 2).reshape(depth * bs * D_MEM, mem_len)
    ckt = cache_k.transpose(0, 2, 3, 1).reshape(n_phys, D_MAIN, page)
    cvt = cache_v.transpose(0, 2, 3, 1).reshape(n_phys, D_MAIN, page)

    convs_p, convs_s, ns_p, ns_s, ms_p, ms_s = [], [], [], [], [], []
    c_p = c_s = None
    c0_p = jnp.zeros((1, bp, H_A, HD_A, HD_A), F32)
    kv = None
    for l in range(depth):
        if l == n_a:
            gain = jnp.concatenate([jnp.tile(g_k_b, H_B), jnp.ones((D_MAIN,), F32)])
            kv = _proj(x, g_kv, w_kv.astype(BF16), gain, tuple(range(D_MAIN // GROUP)))
        x = _ffn(x, g_ffn1[l], wu1, wd1, l)
        wo = w_out[l]
        if l < n_a:
            wi = w_in_a[l]
            q_w, k_w = wi[:, :D_MAIN], wi[:, D_MAIN:2 * D_MAIN]
            v_w, o_w = wi[:, 2 * D_MAIN:3 * D_MAIN], wi[:, 3 * D_MAIN:4 * D_MAIN]
            gate_w = wi[:, 4 * D_MAIN:4 * D_MAIN + 2 * H_A]
            qm_w = wi[:, 4 * D_MAIN + 2 * H_A:]
            w_a = jnp.concatenate([_pad_heads(q_w), _pad_heads(k_w), _pad_heads(v_w), _pad_heads(o_w), qm_w,
                                   jnp.pad(gate_w, ((0, 0), (0, GROUP - 2 * H_A)))], axis=1).astype(BF16)
            n_cols = w_a.shape[1]
            gain = jnp.concatenate([jnp.ones((4 * D_A_PAD,), F32), jnp.tile(g_q_mem[l], H_MEM),
                                    jnp.ones((GROUP,), F32)])
            proj = _proj(x, g_mix[l], w_a, gain, (4 * D_A_PAD // GROUP,), tm=ROW_TILE // 2, head_width=HD_MEM)
            qm_col = 4 * D_A_PAD // D_MEM

            convw = jnp.concatenate([_pad_heads(conv_w_a[l][:, :D_MAIN]), _pad_heads(conv_w_a[l][:, D_MAIN:])], axis=1)
            convb = jnp.concatenate([_pad_heads(conv_b_a[l][:D_MAIN]), _pad_heads(conv_b_a[l][D_MAIN:])]).reshape(1, -1)
            bgate = jnp.pad(b_gate_a[l], (0, GROUP - 2 * H_A)).reshape(1, GROUP)
            gh = _pad_heads(g_h_a[l].reshape(-1)).reshape(1, D_A_PAD)

            def conv_state(c):
                c = jnp.concatenate([_pad_heads(c[..., :D_MAIN]), _pad_heads(c[..., D_MAIN:])], axis=-1)
                return jnp.pad(c, ((0, 0), (8 - (CONV_W - 1), 0), (0, 0)))

            main_p, c_p, n_p, m_p = _mlstm(
                proj, 0, bp, seq, SB_BLOCK, SB_BLOCK,
                conv_state(jnp.zeros((bp, CONV_W - 1, 2 * D_MAIN), F32)), c0_p, 0, c_p, l, n_a,
                jnp.zeros((bp, H_A, HD_A), F32), jnp.zeros((bp, H_A, LANES), F32),
                convw, convb, bgate, gh, BF16)
            main_s, c_s, n_s, m_s = _mlstm(
                proj, tp, bs, lq, lq, LANES, conv_state(state_conv[l]), state_C, l, c_s, l, n_a, state_n[l],
                jnp.broadcast_to(state_m[l][..., None], (bs, H_A, LANES)), convw, convb, bgate, gh, F32)
            w_main = _pad_heads(wo[:D_MAIN].T).T.astype(BF16)

            tail = CONV_W - 1
            conv_rows_p = jnp.stack([proj[(b + 1) * seq - tail:(b + 1) * seq, :2 * D_A_PAD] for b in range(bp)])
            conv_rows_s = proj[tp:].reshape(bs, lq, -1)[:, lq - tail:, :2 * D_A_PAD]
            convs_p.append(_unpad_heads(conv_rows_p, 2 * H_A))
            convs_s.append(_unpad_heads(conv_rows_s, 2 * H_A))
            ns_p.append(n_p); ms_p.append(m_p[..., 0])
            ns_s.append(n_s); ms_s.append(m_s[..., 0])
        else:
            j = l - n_a
            gain = jnp.concatenate([jnp.tile(g_q_b[j], H_B), jnp.tile(g_q_mem[l], H_MEM)])
            proj = _proj(x, g_mix[l], w_in_b[j].astype(BF16), gain, tuple(range(D_MODEL // GROUP)))
            qm_col = D_MAIN // D_MEM
            sb_out = _sb_prompt(proj, kv, b_sb[j], bp, seq, emit_kv_t=(j == 0))
            main_p = sb_out[0]
            if j == 0:
                k_prompt, v_prompt = [a.reshape(bp, H_B, HD_B, seq).transpose(0, 3, 1, 2) for a in sb_out[1:]]
            bias_cols = jnp.pad(jnp.repeat(b_sb[j], lq), (0, page - H_B * lq)).reshape(1, page)
            main_s = _sb_sample(proj, kv, tp, bs, lq, ckt, cvt, page_table, bias_cols)
            w_main = wo[:D_MAIN].astype(BF16)

        mem_p = _memattn(proj, qm_col, 0, bp, seq, 512, mem_kv_p[l], 0, mem_kv_p[l], 1, BF16)
        mem_s = _memattn_small(proj, qm_col, tp, bs, lq, mem_len, cmkt, cmvt, l * bs)
        w_mem = wo[D_MAIN:].astype(BF16)
        mix_p = (main_p, w_main, mem_p, w_mem)
        mix_s = (main_s.astype(BF16), w_main, mem_s.astype(BF16), w_mem)
        if l < depth - 1:
            x = _ffn(x, g_ffn2[l], wu2, wd2, l, row0=0, nrows=tp, mix=mix_p, in_place=True)
            x = _ffn(x, g_ffn2[l], wu2, wd2, l, row0=tp, nrows=ts, mix=mix_s, in_place=True)
        else:
            y_p = _ffn(x, g_ffn2[l], wu2, wd2, l, row0=0, nrows=tp, mix=mix_p)
            y_s = _ffn(x, g_ffn2[l], wu2, wd2, l, row0=tp, nrows=ts, mix=mix_s)

    mem_k_prompt = jnp.stack([m[:, :D_MEM] for m in mem_kv_p]).reshape(depth, bp, mem_len, H_MEM, HD_MEM)
    mem_v_prompt = jnp.stack([m[:, D_MEM:] for m in mem_kv_p]).reshape(depth, bp, mem_len, H_MEM, HD_MEM)
    return (y_p.reshape(bp, seq, d), y_s.reshape(bs, lq, d),
            k_prompt, v_prompt,
            kv[tp:, :D_MAIN].reshape(bs, lq, H_B, HD_B), kv[tp:, D_MAIN:].reshape(bs, lq, H_B, HD_B),
            mem_k_prompt, mem_v_prompt,
            jnp.stack(convs_p), c_p, jnp.stack(ns_p), jnp.stack(ms_p),
            jnp.stack(convs_s), c_s, jnp.stack(ns_s), jnp.stack(ms_s))
```

```python
import functools

import jax
import jax.numpy as jnp
from jax import lax
from jax.experimental import pallas as pl
from jax.experimental.pallas import tpu as pltpu

F32 = jnp.float32
BF16 = jnp.bfloat16

EPS = 1e-6
D_MODEL = 1024
H_MEM, HD_MEM = 4, 64
D_MEM = H_MEM * HD_MEM
D_MAIN = D_MODEL - D_MEM
H_A, HD_A = 4, 192
H_B, HD_B = 12, 64
HD_A_PAD = 256
D_A_PAD = H_A * HD_A_PAD
CONV_W = 4
D_FF = 2816
FF_CHUNK = 256
LANES = 128
GROUP = 256
ROW_TILE = 1024
FFN_ROW_TILE = 1024
VMEM_LIMIT = 56 * 1024 * 1024


def _cparams(sem, vmem=VMEM_LIMIT):
    return pltpu.CompilerParams(dimension_semantics=sem, vmem_limit_bytes=vmem)


def _rms(x, g):
    return x * lax.rsqrt(jnp.mean(x * x, axis=-1, keepdims=True) + EPS) * g


def _sigmoid(x):
    return 1.0 / (1.0 + jnp.exp(-x))


def _softplus(x):
    return jnp.maximum(x, 0.0) + jnp.log(1.0 + jnp.exp(-jnp.abs(x)))


def _head_group_matrix(n, width):
    r = lax.broadcasted_iota(jnp.int32, (n, n), 0) // width
    c = lax.broadcasted_iota(jnp.int32, (n, n), 1) // width
    return (r == c).astype(BF16)


def _ffn_kernel(*refs, mixed):
    if mixed:
        x_ref, a_ref, wa_ref, b_ref, wb_ref, g_ref, wup_ref, wdn_ref, o_ref, h_sc, acc_sc = refs
        o_ref[...] = (x_ref[...]
                      + jnp.dot(a_ref[...], wa_ref[...], preferred_element_type=F32)
                      + jnp.dot(b_ref[...], wb_ref[...], preferred_element_type=F32))
        x_ref = o_ref
    else:
        x_ref, g_ref, wup_ref, wdn_ref, o_ref, h_sc, acc_sc = refs
    h_sc[...] = _rms(x_ref[...], g_ref[...]).astype(BF16)
    d_ff = wdn_ref.shape[1]
    for c in range(d_ff // FF_CHUNK):
        lo, hi = c * FF_CHUNK, (c + 1) * FF_CHUNK
        g = jnp.dot(h_sc[...], wup_ref[0, :, lo:hi], preferred_element_type=F32)
        u = jnp.dot(h_sc[...], wup_ref[0, :, d_ff + lo:d_ff + hi], preferred_element_type=F32)
        act = (g * _sigmoid(g) * u).astype(BF16)
        down = jnp.dot(act, wdn_ref[0, lo:hi, :], preferred_element_type=F32)
        if c == 0:
            acc_sc[...] = down
        else:
            acc_sc[...] += down
    o_ref[...] = x_ref[...] + 0.5 * acc_sc[...]


def _ffn(x, g, w_up, w_down, layer, row0=0, nrows=None, mix=None, in_place=False):
    t, d = x.shape
    nrows = t if nrows is None else nrows
    d_ff = w_down.shape[1]
    tm = FFN_ROW_TILE
    rb0 = row0 // tm
    resident = pl.Buffered(1)
    in_specs = [pl.BlockSpec((tm, d), lambda i: (rb0 + i, 0))]
    args = [x]
    if mix is not None:
        a, wa, b, wb = mix
        ka, kb = a.shape[1], b.shape[1]
        in_specs += [
            pl.BlockSpec((tm, ka), lambda i: (i, 0)),
            pl.BlockSpec((ka, d), lambda i: (0, 0), pipeline_mode=resident),
            pl.BlockSpec((tm, kb), lambda i: (i, 0)),
            pl.BlockSpec((kb, d), lambda i: (0, 0), pipeline_mode=resident),
        ]
        args += [a, wa, b, wb]
    in_specs += [
        pl.BlockSpec((1, d), lambda i: (0, 0)),
        pl.BlockSpec((1, d, 2 * d_ff), lambda i: (layer, 0, 0), pipeline_mode=resident),
        pl.BlockSpec((1, d_ff, d), lambda i: (layer, 0, 0), pipeline_mode=resident),
    ]
    args += [g.reshape(1, d), w_up, w_down]
    out_rb0, out_rows = (rb0, t) if in_place else (0, nrows)
    return pl.pallas_call(
        functools.partial(_ffn_kernel, mixed=mix is not None),
        grid=(nrows // tm,),
        in_specs=in_specs,
        out_specs=pl.BlockSpec((tm, d), lambda i: (out_rb0 + i, 0)),
        out_shape=jax.ShapeDtypeStruct((out_rows, d), F32),
        scratch_shapes=[pltpu.VMEM((tm, d), BF16), pltpu.VMEM((tm, d), F32)],
        input_output_aliases={0: 0} if in_place else {},
        compiler_params=_cparams(("parallel",)),
        name="ffn",
    )(*args)


def _proj_kernel(x_ref, g_ref, w_ref, gain_ref, o_ref, h_sc, *, normed_groups, head_width):
    h_sc[...] = _rms(x_ref[...], g_ref[...]).astype(BF16)
    for gi in range(w_ref.shape[2] // GROUP):
        sl = slice(gi * GROUP, (gi + 1) * GROUP)
        o_ref[:, sl] = jnp.dot(h_sc[...], w_ref[0, :, sl], preferred_element_type=F32)
    squares = [(o_ref[:, gi * GROUP:(gi + 1) * GROUP] ** 2).astype(BF16) for gi in normed_groups]
    sums = [jnp.dot(sq, _head_group_matrix(GROUP, head_width), preferred_element_type=F32) for sq in squares]
    for gi, ss in zip(normed_groups, sums):
        sl = slice(gi * GROUP, (gi + 1) * GROUP)
        o_ref[:, sl] = o_ref[:, sl] * lax.rsqrt(ss * (1.0 / head_width) + EPS) * gain_ref[:, sl]


def _proj(x, g, w, layer, gain, normed_groups, tm=ROW_TILE, head_width=HD_B):
    t, d = x.shape
    n = w.shape[2]
    return pl.pallas_call(
        functools.partial(_proj_kernel, normed_groups=tuple(normed_groups), head_width=head_width),
        grid=(t // tm,),
        in_specs=[
            pl.BlockSpec((tm, d), lambda i: (i, 0)),
            pl.BlockSpec((1, d), lambda i: (0, 0)),
            pl.BlockSpec((1, d, n), lambda i: (layer, 0, 0), pipeline_mode=pl.Buffered(1)),
            pl.BlockSpec((1, n), lambda i: (0, 0)),
        ],
        out_specs=pl.BlockSpec((tm, n), lambda i: (i, 0)),
        out_shape=jax.ShapeDtypeStruct((t, n), F32),
        scratch_shapes=[pltpu.VMEM((tm, d), BF16)],
        compiler_params=_cparams(("parallel",)),
        name="proj",
    )(x, g.reshape(1, d), w, gain.reshape(1, n))


def _memattn_kernel(q_ref, k_ref, v_ref, o_ref):
    q = q_ref[...]
    kb = k_ref[...].astype(BF16)
    vb = v_ref[...].astype(BF16)
    lane = lax.broadcasted_iota(jnp.int32, q.shape, 1) // HD_MEM
    scores = []
    for h in range(H_MEM):
        qh = jnp.where(lane == h, q, 0.0).astype(BF16)
        scores.append(lax.dot_general(qh, kb, (((1,), (1,)), ((), ())), preferred_element_type=F32)
                      * (HD_MEM ** -0.5))
    probs = []
    for s in scores:
        e = jnp.exp(s - jnp.max(s, axis=-1, keepdims=True))
        probs.append((e / jnp.sum(e, axis=-1, keepdims=True)).astype(BF16))
    acc = jnp.zeros(q.shape, F32)
    for h, p in enumerate(probs):
        acc = acc + jnp.where(lane == h, jnp.dot(p, vb, preferred_element_type=F32), 0.0)
    o_ref[...] = acc.astype(o_ref.dtype)


def _memattn(qarr, qcol, row0, nb, lq, tq, karr, kcol, varr, vcol, out_dtype):
    m = karr.shape[0] // nb
    nq = lq // tq
    rb0 = row0 // tq
    return pl.pallas_call(
        _memattn_kernel,
        grid=(nb, nq),
        in_specs=[
            pl.BlockSpec((tq, D_MEM), lambda b, i: (rb0 + b * nq + i, qcol)),
            pl.BlockSpec((m, D_MEM), lambda b, i: (b, kcol)),
            pl.BlockSpec((m, D_MEM), lambda b, i: (b, vcol)),
        ],
        out_specs=pl.BlockSpec((tq, D_MEM), lambda b, i: (b * nq + i, 0)),
        out_shape=jax.ShapeDtypeStruct((nb * lq, D_MEM), out_dtype),
        compiler_params=_cparams(("parallel", "parallel")),
        name="memattn",
    )(qarr, karr, varr)


def _memattn_small_kernel(q_ref, k_ref, v_ref, o_ref, *, nseq, lq):
    lane = lax.broadcasted_iota(jnp.int32, (lq, D_MEM), 1) // HD_MEM
    scores = []
    for s in range(nseq):
        q = q_ref[s * lq:(s + 1) * lq, :]
        qx = jnp.concatenate([jnp.where(lane == h, q, 0.0) for h in range(H_MEM)], axis=0).astype(BF16)
        kt = k_ref[s * D_MEM:(s + 1) * D_MEM, :].astype(BF16)
        scores.append(jnp.dot(qx, kt, preferred_element_type=F32) * (HD_MEM ** -0.5))
    probs = []
    for sc in scores:
        e = jnp.exp(sc - jnp.max(sc, axis=-1, keepdims=True))
        probs.append((e / jnp.sum(e, axis=-1, keepdims=True)).astype(BF16))
    for s, p in enumerate(probs):
        vt = v_ref[s * D_MEM:(s + 1) * D_MEM, :].astype(BF16)
        ox = lax.dot_general(p, vt, (((1,), (1,)), ((), ())), preferred_element_type=F32)
        out = jnp.zeros((lq, D_MEM), F32)
        for h in range(H_MEM):
            out = out + jnp.where(lane == h, ox[h * lq:(h + 1) * lq, :], 0.0)
        o_ref[s * lq:(s + 1) * lq, :] = out


def _memattn_small(qarr, qcol, row0, nb, lq, m, ktarr, vtarr, seq0, nseq=8):
    rb0 = row0 // (nseq * lq)
    kb0 = seq0 // nseq
    return pl.pallas_call(
        functools.partial(_memattn_small_kernel, nseq=nseq, lq=lq),
        grid=(nb // nseq,),
        in_specs=[
            pl.BlockSpec((nseq * lq, D_MEM), lambda i: (rb0 + i, qcol)),
            pl.BlockSpec((nseq * D_MEM, m), lambda i: (kb0 + i, 0)),
            pl.BlockSpec((nseq * D_MEM, m), lambda i: (kb0 + i, 0)),
        ],
        out_specs=pl.BlockSpec((nseq * lq, D_MEM), lambda i: (i, 0)),
        out_shape=jax.ShapeDtypeStruct((nb * lq, D_MEM), F32),
        compiler_params=_cparams(("parallel",)),
        name="memattn_small",
    )(qarr, ktarr, vtarr)


def _mlstm_kernel(qkvo_ref, gates_ref, convprev_ref, c0_ref, n0_ref, m0_ref,
                  convw_ref, convb_ref, bgate_ref, gh_ref, c_all_ref,
                  h_ref, c_out_ref, n_out_ref, m_out_ref,
                  ext_sc, c_sc, n_sc, m_sc, *, lb, lp, slot):
    ci = pl.program_id(1)
    last = lb - 1

    @pl.when(ci == 0)
    def _():
        ext_sc[0:8, :] = convprev_ref[0]
        c_sc[...] = jnp.zeros_like(c_sc)
        n_sc[...] = jnp.zeros_like(n_sc)
        for h in range(H_A):
            c_sc[h, 0:HD_A, 0:HD_A] = c0_ref[0, 0, h]
        n_sc[0:H_A, 0:HD_A] = n0_ref[0]
        m_sc[0:H_A, :] = m0_ref[0]

    def pad_rows(a):
        return a if lb == lp else jnp.concatenate([a, jnp.zeros((lp - lb, a.shape[1]), a.dtype)], axis=0)

    ext_sc[8:8 + lb, :] = qkvo_ref[:, 0:2 * D_A_PAD]

    row = lax.broadcasted_iota(jnp.int32, (lp, lp), 0)
    col = lax.broadcasted_iota(jnp.int32, (lp, lp), 1)
    causal = row >= col
    valid_row = lax.broadcasted_iota(jnp.int32, (lp, 1), 0) < lb

    gl = lax.broadcasted_iota(jnp.int32, (lp, GROUP), 1)
    gts = pad_rows(gates_ref[...]) + bgate_ref[...]
    gts = jnp.where((gl >= H_A) & (gl < 2 * H_A), -_softplus(-gts), gts)
    gts = jnp.where(valid_row & (gl < 2 * H_A), gts, 0.0)
    cum = jnp.dot(causal.astype(F32), gts, preferred_element_type=F32, precision=lax.Precision.HIGHEST)
    gts_t = gts.T
    cum_t = cum.T

    def conv_silu(c0):
        sl = slice(c0, c0 + HD_A_PAD)
        a = convb_ref[:, sl]
        for j in range(CONV_W):
            a = a + ext_sc[pl.ds(8 - (CONV_W - 1) + j, lb), sl] * convw_ref[j:j + 1, sl]
        return pad_rows(a * _sigmoid(a))

    heads = []
    for h in range(H_A):
        q = conv_silu(h * HD_A_PAD)
        k = conv_silu(D_A_PAD + h * HD_A_PAD) * (HD_A ** -0.5)
        vb = pad_rows(qkvo_ref[:, 2 * D_A_PAD + h * HD_A_PAD:2 * D_A_PAD + (h + 1) * HD_A_PAD]).astype(BF16)
        qb, kb = q.astype(BF16), k.astype(BF16)
        s = lax.dot_general(qb, kb, (((1,), (1,)), ((), ())), preferred_element_type=F32)
        cq = lax.dot_general(qb, c_sc[h].astype(BF16), (((1,), (1,)), ((), ())), preferred_element_type=F32)
        qn = jnp.sum(q * n_sc[h:h + 1, :], axis=-1, keepdims=True)
        heads.append((k, vb, s, cq, qn))

    for h, (k, vb, s, cq, qn) in enumerate(heads):
        hs = slice(h * HD_A_PAD, (h + 1) * HD_A_PAD)
        o_pre = qkvo_ref[:, 3 * D_A_PAD + h * HD_A_PAD:3 * D_A_PAD + (h + 1) * HD_A_PAD]

        ig_col = gts[:, h:h + 1]
        b_col = cum[:, H_A + h:H_A + h + 1]
        ig_row = gts_t[h:h + 1, :]
        b_row = cum_t[H_A + h:H_A + h + 1, :]
        m_prev = m_sc[h:h + 1, 0:1]
        c_prev = c_sc[h]
        n_prev = n_sc[h:h + 1, :]

        dlog = jnp.where(causal, b_col - b_row + ig_row, -jnp.inf)
        inter = b_col + m_prev
        m_t = jnp.maximum(inter, jnp.max(dlog, axis=-1, keepdims=True))
        w = jnp.exp(dlog - m_t) * s
        w_inter = jnp.exp(inter - m_t)
        num = jnp.dot(w.astype(BF16), vb, preferred_element_type=F32) + cq * w_inter
        den = jnp.sum(w, axis=-1, keepdims=True) + w_inter * qn
        den = jnp.maximum(jnp.abs(den), jnp.exp(-m_t))
        hh = num / den
        hn = hh * lax.rsqrt(jnp.sum(hh * hh, axis=-1, keepdims=True) * (1.0 / HD_A) + EPS) * gh_ref[:, hs]
        h_ref[:, hs] = (hn[0:lb, :] * _sigmoid(o_pre)).astype(h_ref.dtype)

        m_new = m_t[last:last + 1, :]
        b_last = b_col[last:last + 1, :]
        w_end = jnp.exp(b_last - b_col + ig_col - m_new)
        decay = jnp.exp(b_last + m_prev - m_new)
        kw = k * w_end
        c_sc[h] = decay * c_prev + lax.dot_general(vb, kw.astype(BF16), (((0,), (0,)), ((), ())),
                                                   preferred_element_type=F32)
        n_sc[h:h + 1, :] = decay * n_prev + jnp.sum(kw, axis=0, keepdims=True)
        m_sc[h:h + 1, :] = jnp.broadcast_to(m_new, (1, LANES))

    if lb == lp:
        ext_sc[0:8, :] = ext_sc[lp:lp + 8, :]

    @pl.when(ci == pl.num_programs(1) - 1)
    def _():
        for s in range(c_out_ref.shape[0]):
            if s != slot:
                c_out_ref[s] = jnp.zeros(c_out_ref.shape[1:], F32)
        for h in range(H_A):
            c_out_ref[slot, 0, h] = c_sc[h, 0:HD_A, 0:HD_A]
        n_out_ref[0] = n_sc[0:H_A, 0:HD_A]
        m_out_ref[0] = m_sc[0:H_A, :]


def _mlstm(proj, row0, nb, seq, lb, lp, convprev, c0, c0_slot, c_out_prev, layer, n_layers, n0, m0,
           convw, convb, bgate, gh, out_dtype):
    creates = c_out_prev is None
    nc = seq // lb
    rb0 = row0 // lb
    gate_col = (4 * D_A_PAD + D_MEM) // GROUP
    scratch = [
        pltpu.VMEM((lp + 8, 2 * D_A_PAD), F32),
        pltpu.VMEM((H_A, HD_A_PAD, HD_A_PAD), F32),
        pltpu.VMEM((8, HD_A_PAD), F32),
        pltpu.VMEM((8, LANES), F32),
    ]
    assert lb == lp or nc == 1, "a zero-padded tile must be the only chunk of its sequence"
    in_specs = [
        pl.BlockSpec((lb, 4 * D_A_PAD), lambda b, c: (rb0 + b * nc + c, 0)),
        pl.BlockSpec((lb, GROUP), lambda b, c: (rb0 + b * nc + c, gate_col)),
        pl.BlockSpec((1, 8, 2 * D_A_PAD), lambda b, c: (b, 0, 0)),
        pl.BlockSpec((1, 1, H_A, HD_A, HD_A), lambda b, c: (c0_slot, b, 0, 0, 0)),
        pl.BlockSpec((1, H_A, HD_A), lambda b, c: (b, 0, 0)),
        pl.BlockSpec((1, H_A, LANES), lambda b, c: (b, 0, 0)),
        pl.BlockSpec((CONV_W, 2 * D_A_PAD), lambda b, c: (0, 0)),
        pl.BlockSpec((1, 2 * D_A_PAD), lambda b, c: (0, 0)),
        pl.BlockSpec((1, GROUP), lambda b, c: (0, 0)),
        pl.BlockSpec((1, D_A_PAD), lambda b, c: (0, 0)),
    ]
    in_specs.append(pl.BlockSpec(memory_space=pl.ANY))
    args = [proj, proj, convprev, c0, n0, m0, convw, convb, bgate, gh, c0 if creates else c_out_prev]
    aliases = {} if creates else {len(args) - 1: 1}
    c_block = (n_layers, 0) if creates else (1, layer)
    return pl.pallas_call(
        functools.partial(_mlstm_kernel, lb=lb, lp=lp, slot=layer if creates else 0),
        grid=(nb, nc),
        in_specs=in_specs,
        out_specs=[
            pl.BlockSpec((lb, D_A_PAD), lambda b, c: (b * nc + c, 0)),
            pl.BlockSpec((c_block[0], 1, H_A, HD_A, HD_A), lambda b, c: (c_block[1], b, 0, 0, 0)),
            pl.BlockSpec((1, H_A, HD_A), lambda b, c: (b, 0, 0)),
            pl.BlockSpec((1, H_A, LANES), lambda b, c: (b, 0, 0)),
        ],
        out_shape=[
            jax.ShapeDtypeStruct((nb * seq, D_A_PAD), out_dtype),
            jax.ShapeDtypeStruct((n_layers, nb, H_A, HD_A, HD_A), F32),
            jax.ShapeDtypeStruct((nb, H_A, HD_A), F32),
            jax.ShapeDtypeStruct((nb, H_A, LANES), F32),
        ],
        scratch_shapes=scratch,
        input_output_aliases=aliases,
        compiler_params=_cparams(("parallel", "arbitrary")),
        name="mlstm",
    )(*args)


SB_BLOCK = 256
SB_QBLOCK = 1024
LOG2E = 1.4426950408889634


def _neg_later(n, keys_on_rows):
    r = lax.broadcasted_iota(jnp.int32, (n, n), 0)
    c = lax.broadcasted_iota(jnp.int32, (n, n), 1)
    later = (c > r) if keys_on_rows else (r > c)
    return jnp.where(later, -1.0, 0.0).astype(BF16)


def _softplus2(z):
    return jnp.where(z > 64.0, z, jnp.log(1.0 + jnp.exp2(z)) * LOG2E)


def _sb_weights(z_streams, masks, later_mat, carries, keys_on_rows=False, row0s=None):
    first = (lambda x: x[0:1, :]) if keys_on_rows else (lambda x: x[:, 0:1])
    row0s = [0] * len(masks) if row0s is None else row0s

    own, sp_first, inblock = [], [], []
    for zs in z_streams:
        o_s, f_s, i_s = [], [], []
        for z, m in zip(zs, masks):
            sp = _softplus2(z) if m is None else jnp.where(m, _softplus2(z), 0.0)
            o_s.append(z - sp)
            f_s.append(first(sp))
            spb = sp.astype(BF16)
            i_s.append(jnp.dot(later_mat, spb, preferred_element_type=F32) if keys_on_rows
                       else jnp.dot(spb, later_mat, preferred_element_type=F32))
        own.append(o_s)
        sp_first.append(f_s)
        inblock.append(i_s)
    weights, new_carries = [], []
    for o_s, f_s, i_s, carry in zip(own, sp_first, inblock, carries):
        ws = []
        for o, f, l, m, r0 in zip(o_s, f_s, i_s, masks, row0s):
            later = l + carry[r0:]
            a = jnp.exp2(o + later)
            ws.append(a if m is None else jnp.where(m, a, 0.0))
            new = first(later) - f
            carry = new if r0 == 0 else jnp.concatenate([carry[:r0], new], axis=0)
        weights.append(ws)
        new_carries.append(carry)
    return weights, tuple(new_carries)


def _sb_prompt_kernel(bias_ref, q_ref, k_ref, v_ref, o_ref, *rest, emit_kv_t):
    p = pl.program_id(1)
    i = pl.program_id(2)
    blk = SB_BLOCK
    nh = LANES // HD_B
    if emit_kv_t:
        kt_ref, vt_ref = rest[:2]
        rest = rest[2:]
    k_scs, v_scs = rest[:nh], rest[nh:]
    spare = [((hh + 1) % nh) * HD_B for hh in range(nh)]

    @pl.when(i == 0)
    def _():
        k = k_ref[...]
        v = v_ref[...]
        lane_id = lax.broadcasted_iota(jnp.int32, k.shape, 1)
        for hh in range(nh):
            ones = (lane_id == spare[hh]) | (lane_id == spare[hh] + 1)
            k_scs[hh][...] = jnp.where(lane_id // HD_B == hh, k, jnp.where(ones, 1.0, 0.0)).astype(BF16)
            v_scs[hh][...] = jnp.where(lane_id // HD_B == hh, v, 0.0).astype(BF16)
        if emit_kv_t:
            kt_ref[0] = k.T
            vt_ref[0] = v.T

    q = q_ref[...]
    tq = q.shape[0]
    per_q = tq // blk
    lane_id = lax.broadcasted_iota(jnp.int32, q.shape, 1)
    later_mat = _neg_later(blk, keys_on_rows=False)
    qh = []
    for hh in range(nh):
        b2 = jnp.full((1, LANES), bias_ref[p * nh + hh] * LOG2E, F32)
        b2_hi = b2.astype(BF16).astype(F32)
        extra = jnp.where(lane_id == spare[hh], b2_hi, jnp.where(lane_id == spare[hh] + 1, b2 - b2_hi, 0.0))
        qh.append(jnp.where(lane_id // HD_B == hh, q * (HD_B ** -0.5 * LOG2E), extra).astype(BF16))

    def tiles(js, masks, row0s, carries, acc):
        offs = [pl.multiple_of(j * blk, blk) for j in js]
        zs = [[lax.dot_general(qh[hh][r0:], k_scs[hh][pl.ds(off, blk), :], (((1,), (1,)), ((), ())),
                               preferred_element_type=F32) for off, r0 in zip(offs, row0s)] for hh in range(nh)]
        ws, carries = _sb_weights(zs, masks, later_mat, carries, row0s=row0s)
        for hh in range(nh):
            for off, a, r0 in zip(offs, ws[hh], row0s):
                upd = jnp.dot(a.astype(BF16), v_scs[hh][pl.ds(off, blk), :], preferred_element_type=F32)
                acc = acc + (upd if r0 == 0 else jnp.concatenate([jnp.zeros((r0, LANES), F32), upd], axis=0))
        return carries, acc

    rel = lax.broadcasted_iota(jnp.int32, (tq, blk), 0) - lax.broadcasted_iota(jnp.int32, (tq, blk), 1)
    first = i * per_q
    order = list(reversed(range(per_q)))
    zero_c = tuple(jnp.zeros((tq, 1), F32) for _ in range(nh))
    carries, acc = tiles([first + t for t in order], [rel[t * blk:] > t * blk for t in order],
                         [t * blk for t in order], zero_c, jnp.zeros(q.shape, F32))

    def body(t, state):
        j = first - 1 - t * per_q
        return tiles([j - u for u in range(per_q)], [None] * per_q, [0] * per_q, state[0], state[1])

    _, acc = lax.fori_loop(0, i, body, (carries, acc))
    o_ref[...] = acc.astype(o_ref.dtype)


def _sb_prompt(qarr, kvarr, bias, nb, seq, emit_kv_t):
    blk = SB_QBLOCK
    npair = D_MAIN // LANES
    ni = seq // blk
    out_specs = [pl.BlockSpec((blk, LANES), lambda b, p, i, s: (b * ni + i, p))]
    out_shape = [jax.ShapeDtypeStruct((nb * seq, D_MAIN), BF16)]
    if emit_kv_t:
        out_specs += [pl.BlockSpec((1, LANES, seq), lambda b, p, i, s: (b, p, 0))] * 2
        out_shape += [jax.ShapeDtypeStruct((nb, D_MAIN, seq), F32)] * 2
    grid_spec = pltpu.PrefetchScalarGridSpec(
        num_scalar_prefetch=1,
        grid=(nb, npair, ni),
        in_specs=[
            pl.BlockSpec((blk, LANES), lambda b, p, i, s: (b * ni + i, p)),
            pl.BlockSpec((seq, LANES), lambda b, p, i, s: (b, p)),
            pl.BlockSpec((seq, LANES), lambda b, p, i, s: (b, npair + p)),
        ],
        out_specs=out_specs,
        scratch_shapes=[pltpu.VMEM((seq, LANES), BF16)] * (2 * LANES // HD_B),
    )
    return pl.pallas_call(
        functools.partial(_sb_prompt_kernel, emit_kv_t=emit_kv_t),
        grid_spec=grid_spec,
        out_shape=out_shape,
        compiler_params=_cparams(("parallel", "parallel", "arbitrary")),
        name="sb_prompt",
    )(bias, qarr, kvarr, kvarr)


def _sb_sample_kernel(pt_ref, q_ref, knew_ref, vnew_ref, *refs, lq, page, npage):
    kt_refs, vt_refs = refs[:npage], refs[npage:2 * npage]
    bias_ref, o_ref, pad_sc = refs[2 * npage:]
    rows = H_B * lq
    later_mat = _neg_later(page, keys_on_rows=True)
    tn_dims = (((0,), (0,)), ((), ()))

    q = q_ref[...]
    lane = lax.broadcasted_iota(jnp.int32, q.shape, 1) // HD_B
    qx = jnp.concatenate([jnp.where(lane == h, q, 0.0) for h in range(H_B)]
                         + [jnp.zeros((page - rows, D_MAIN), F32)], axis=0)
    qxt = (qx * (HD_B ** -0.5 * LOG2E)).T.astype(BF16)
    bias = bias_ref[...] * LOG2E

    pad_sc[...] = jnp.zeros_like(pad_sc)
    pad_sc[0:lq, :] = knew_ref[...]
    knew = pad_sc[...].astype(BF16)
    pad_sc[0:lq, :] = vnew_ref[...]
    vnew = pad_sc[...].astype(BF16)
    kidx = lax.broadcasted_iota(jnp.int32, (page, page), 0)
    qidx = lax.broadcasted_iota(jnp.int32, (page, page), 1) % lq
    order = list(reversed(range(npage)))

    zs = [jnp.dot(knew, qxt, preferred_element_type=F32) + bias]
    zs += [lax.dot_general(kt_refs[g][0].astype(BF16), qxt, tn_dims, preferred_element_type=F32) + bias
           for g in order]
    masks = [kidx < qidx] + [None] * npage
    (ws,), _ = _sb_weights([zs], masks, later_mat, (jnp.zeros((1, page), F32),), keys_on_rows=True)

    acc = lax.dot_general(vnew, ws[0].astype(BF16), tn_dims, preferred_element_type=F32)
    for g, a in zip(order, ws[1:]):
        acc = acc + jnp.dot(vt_refs[g][0].astype(BF16), a.astype(BF16), preferred_element_type=F32)

    acc_t = acc.T
    olane = lax.broadcasted_iota(jnp.int32, (lq, D_MAIN), 1) // HD_B
    out = jnp.zeros((lq, D_MAIN), F32)
    for h in range(H_B):
        out = out + jnp.where(olane == h, acc_t[h * lq:(h + 1) * lq, :], 0.0)
    o_ref[...] = out


def _sb_sample(qarr, kvarr, row0, nb, lq, cache_kt, cache_vt, page_table, bias_cols):
    npage = page_table.shape[1]
    page = cache_kt.shape[2]
    rb0 = row0 // lq
    page_specs = [pl.BlockSpec((1, D_MAIN, page), lambda b, pt, g=g: (pt[b, g], 0, 0)) for g in range(npage)]
    grid_spec = pltpu.PrefetchScalarGridSpec(
        num_scalar_prefetch=1,
        grid=(nb,),
        in_specs=[
            pl.BlockSpec((lq, D_MAIN), lambda b, pt: (rb0 + b, 0)),
            pl.BlockSpec((lq, D_MAIN), lambda b, pt: (rb0 + b, 0)),
            pl.BlockSpec((lq, D_MAIN), lambda b, pt: (rb0 + b, 1)),
        ] + page_specs + page_specs + [pl.BlockSpec((1, page), lambda b, pt: (0, 0))],
        out_specs=pl.BlockSpec((lq, D_MAIN), lambda b, pt: (b, 0)),
        scratch_shapes=[pltpu.VMEM((page, D_MAIN), F32)],
    )
    return pl.pallas_call(
        functools.partial(_sb_sample_kernel, lq=lq, page=page, npage=npage),
        grid_spec=grid_spec,
        out_shape=jax.ShapeDtypeStruct((nb * lq, D_MAIN), F32),
        compiler_params=_cparams(("parallel",)),
        name="sb_sample",
    )(page_table, qarr, kvarr, kvarr, *([cache_kt] * npage), *([cache_vt] * npage), bias_cols)


def _pad_heads(w):
    lead, nh = w.shape[:-1], w.shape[-1] // HD_A
    w = jnp.pad(w.reshape(lead + (nh, HD_A)), [(0, 0)] * (len(lead) + 1) + [(0, HD_A_PAD - HD_A)])
    return w.reshape(lead + (nh * HD_A_PAD,))


def _unpad_heads(w, nh):
    lead = w.shape[:-1]
    return w.reshape(lead + (nh, HD_A_PAD))[..., :HD_A].reshape(lead + (nh * HD_A,))


def kernel(x_prompt, x_sample, cache_k, cache_v, cache_mem_k, cache_mem_v, state_C, state_n, state_m, state_conv, page_table, mem_prompt, g_ffn1, w_ffn1_up, w_ffn1_down, g_mix, w_in_a, b_gate_a, conv_w_a, conv_b_a, g_h_a, g_kv, w_kv, g_k_b, w_in_b, g_q_b, b_sb, g_mem, w_mem_kv, g_q_mem, g_k_mem, w_out, g_ffn2, w_ffn2_up, w_ffn2_down):
    bp, seq, d = x_prompt.shape
    bs, lq, _ = x_sample.shape
    depth = g_ffn1.shape[0]
    n_a = w_in_a.shape[0]
    tp, ts = bp * seq, bs * lq
    mem_len = mem_prompt.shape[1]
    n_phys, page = cache_k.shape[0], cache_k.shape[1]

    x = jnp.concatenate([x_prompt.reshape(tp, d), x_sample.reshape(bs * lq, d)], axis=0)
    wu1, wd1 = w_ffn1_up.astype(BF16), w_ffn1_down.astype(BF16)
    wu2, wd2 = w_ffn2_up.astype(BF16), w_ffn2_down.astype(BF16)
    wo_b, w_kv_b, w_in_b_b, w_mem_b = (w.astype(BF16) for w in (w_out, w_kv[None], w_in_b, w_mem_kv))
    split = 4 * D_MAIN
    w_a_all = jnp.concatenate([
        _pad_heads(w_in_a[:, :, :split]), w_in_a[:, :, split + 2 * H_A:],
        jnp.pad(w_in_a[:, :, split:split + 2 * H_A], ((0, 0), (0, 0), (0, GROUP - 2 * H_A)))], axis=2).astype(BF16)
    w_main_a = jnp.pad(wo_b[:n_a, :D_MAIN].reshape(n_a, H_A, HD_A, d),
                       ((0, 0), (0, 0), (0, HD_A_PAD - HD_A), (0, 0))).reshape(n_a, D_A_PAD, d)
    convw_all, convb_all = _pad_heads(conv_w_a), _pad_heads(conv_b_a)[:, None, :]
    bgate_all = jnp.pad(b_gate_a, ((0, 0), (0, GROUP - 2 * H_A)))[:, None, :]
    gh_all = _pad_heads(g_h_a.reshape(n_a, H_A * HD_A))[:, None, :]

    mem_rows = mem_prompt.reshape(bp * mem_len, d)
    mem_kv_p = []
    for l in range(depth):
        gain = jnp.concatenate([jnp.tile(g_k_mem[l], H_MEM), jnp.ones((D_MEM,), F32)])
        mem_kv_p.append(_proj(mem_rows, g_mem[l], w_mem_b, l, gain, (0,), tm=bp * mem_len, head_width=HD_MEM))
    cmkt = cache_mem_k.transpose(0, 1, 3, 4, 2).reshape(depth * bs * D_MEM, mem_len)
    cmvt = cache_mem_v.transpose(0, 1, 3, 4, 2).reshape(depth * bs * D_MEM, mem_len)
    ckt = cache_k.transpose(0, 2, 3, 1).reshape(n_phys, D_MAIN, page)
    cvt = cache_v.transpose(0, 2, 3, 1).reshape(n_phys, D_MAIN, page)

    convs_p, convs_s, ns_p, ns_s, ms_p, ms_s = [], [], [], [], [], []
    c_p = c_s = None
    c0_p = jnp.zeros((1, bp, H_A, HD_A, HD_A), F32)
    kv = None
    for l in range(depth):
        if l == n_a:
            gain = jnp.concatenate([jnp.tile(g_k_b, H_B), jnp.ones((D_MAIN,), F32)])
            kv = _proj(x, g_kv, w_kv_b, 0, gain, tuple(range(D_MAIN // GROUP)))
        x = _ffn(x, g_ffn1[l], wu1, wd1, l)
        if l < n_a:
            gain = jnp.concatenate([jnp.ones((4 * D_A_PAD,), F32), jnp.tile(g_q_mem[l], H_MEM),
                                    jnp.ones((GROUP,), F32)])
            proj = _proj(x, g_mix[l], w_a_all, l, gain, (4 * D_A_PAD // GROUP,), tm=ROW_TILE // 2,
                         head_width=HD_MEM)
            qm_col = 4 * D_A_PAD // D_MEM
            convw, convb, bgate, gh = convw_all[l], convb_all[l], bgate_all[l], gh_all[l]

            def conv_state(c):
                return jnp.pad(_pad_heads(c), ((0, 0), (8 - (CONV_W - 1), 0), (0, 0)))

            main_p, c_p, n_p, m_p = _mlstm(
                proj, 0, bp, seq, SB_BLOCK, SB_BLOCK,
                conv_state(jnp.zeros((bp, CONV_W - 1, 2 * D_MAIN), F32)), c0_p, 0, c_p, l, n_a,
                jnp.zeros((bp, H_A, HD_A), F32), jnp.zeros((bp, H_A, LANES), F32),
                convw, convb, bgate, gh, BF16)
            main_s, c_s, n_s, m_s = _mlstm(
                proj, tp, bs, lq, lq, LANES, conv_state(state_conv[l]), state_C, l, c_s, l, n_a, state_n[l],
                jnp.broadcast_to(state_m[l][..., None], (bs, H_A, LANES)), convw, convb, bgate, gh, F32)
            w_main = w_main_a[l]

            tail = CONV_W - 1
            conv_rows_p = jnp.stack([proj[(b + 1) * seq - tail:(b + 1) * seq, :2 * D_A_PAD] for b in range(bp)])
            conv_rows_s = proj[tp:].reshape(bs, lq, -1)[:, lq - tail:, :2 * D_A_PAD]
            convs_p.append(_unpad_heads(conv_rows_p, 2 * H_A))
            convs_s.append(_unpad_heads(conv_rows_s, 2 * H_A))
            ns_p.append(n_p); ms_p.append(m_p[..., 0])
            ns_s.append(n_s); ms_s.append(m_s[..., 0])
        else:
            j = l - n_a
            gain = jnp.concatenate([jnp.tile(g_q_b[j], H_B), jnp.tile(g_q_mem[l], H_MEM)])
            proj = _proj(x, g_mix[l], w_in_b_b, j, gain, tuple(range(D_MODEL // GROUP)))
            qm_col = D_MAIN // D_MEM
            sb_out = _sb_prompt(proj, kv, b_sb[j], bp, seq, emit_kv_t=(j == 0))
            main_p = sb_out[0]
            if j == 0:
                k_prompt, v_prompt = [a.reshape(bp, H_B, HD_B, seq).transpose(0, 3, 1, 2) for a in sb_out[1:]]
            bias_cols = jnp.pad(jnp.repeat(b_sb[j], lq), (0, page - H_B * lq)).reshape(1, page)
            main_s = _sb_sample(proj, kv, tp, bs, lq, ckt, cvt, page_table, bias_cols)
            w_main = wo_b[l, :D_MAIN]

        mem_p = _memattn(proj, qm_col, 0, bp, seq, 512, mem_kv_p[l], 0, mem_kv_p[l], 1, BF16)
        mem_s = _memattn_small(proj, qm_col, tp, bs, lq, mem_len, cmkt, cmvt, l * bs)
        w_mem = wo_b[l, D_MAIN:]
        mix_p = (main_p, w_main, mem_p, w_mem)
        mix_s = (main_s.astype(BF16), w_main, mem_s.astype(BF16), w_mem)
        if l < depth - 1:
            x = _ffn(x, g_ffn2[l], wu2, wd2, l, row0=0, nrows=tp, mix=mix_p, in_place=True)
            x = _ffn(x, g_ffn2[l], wu2, wd2, l, row0=tp, nrows=ts, mix=mix_s, in_place=True)
        else:
            y_p = _ffn(x, g_ffn2[l], wu2, wd2, l, row0=0, nrows=tp, mix=mix_p)
            y_s = _ffn(x, g_ffn2[l], wu2, wd2, l, row0=tp, nrows=ts, mix=mix_s)

    mem_k_prompt = jnp.stack([m[:, :D_MEM] for m in mem_kv_p]).reshape(depth, bp, mem_len, H_MEM, HD_MEM)
    mem_v_prompt = jnp.stack([m[:, D_MEM:] for m in mem_kv_p]).reshape(depth, bp, mem_len, H_MEM, HD_MEM)
    return (y_p.reshape(bp, seq, d), y_s.reshape(bs, lq, d),
            k_prompt, v_prompt,
            kv[tp:, :D_MAIN].reshape(bs, lq, H_B, HD_B), kv[tp:, D_MAIN:].reshape(bs, lq, H_B, HD_B),
            mem_k_prompt, mem_v_prompt,
            jnp.stack(convs_p), c_p, jnp.stack(ns_p), jnp.stack(ms_p),
            jnp.stack(convs_s), c_s, jnp.stack(ns_s), jnp.stack(ms_s))
```

```python
import functools

import jax
import jax.numpy as jnp
from jax import lax
from jax.experimental import pallas as pl
from jax.experimental.pallas import tpu as pltpu

F32 = jnp.float32
BF16 = jnp.bfloat16

EPS = 1e-6
D_MODEL = 1024
H_MEM, HD_MEM = 4, 64
D_MEM = H_MEM * HD_MEM
D_MAIN = D_MODEL - D_MEM
H_A, HD_A = 4, 192
H_B, HD_B = 12, 64
HD_A_PAD = 256
D_A_PAD = H_A * HD_A_PAD
CONV_W = 4
D_FF = 2816
FF_CHUNK = 256
LANES = 128
GROUP = 256
ROW_TILE = 1024
FFN_ROW_TILE = 1024
VMEM_LIMIT = 56 * 1024 * 1024


def _cparams(sem, vmem=VMEM_LIMIT):
    return pltpu.CompilerParams(dimension_semantics=sem, vmem_limit_bytes=vmem)


def _rms(x, g):
    return x * lax.rsqrt(jnp.mean(x * x, axis=-1, keepdims=True) + EPS) * g


def _sigmoid(x):
    return 1.0 / (1.0 + jnp.exp(-x))


def _softplus(x):
    return jnp.maximum(x, 0.0) + jnp.log(1.0 + jnp.exp(-jnp.abs(x)))


def _head_group_matrix(n, width):
    r = lax.broadcasted_iota(jnp.int32, (n, n), 0) // width
    c = lax.broadcasted_iota(jnp.int32, (n, n), 1) // width
    return (r == c).astype(BF16)


def _ffn_kernel(*refs, mixed, first_tiles=None):
    if mixed:
        x_ref, a_ref, wa_ref, b_ref, wb_ref = refs[:5]
        refs = refs[5:]
        if first_tiles is None:
            srcs = [(None, a_ref, b_ref)]
        else:
            a2_ref, b2_ref = refs[:2]
            refs = refs[2:]
            tile = pl.program_id(0)
            srcs = [(tile < first_tiles, a_ref, b_ref), (tile >= first_tiles, a2_ref, b2_ref)]
        g_ref, wup_ref, wdn_ref, o_ref, h_sc, acc_sc = refs
        for cond, a, b in srcs:
            def add_mix(a=a, b=b):
                o_ref[...] = (x_ref[...]
                              + jnp.dot(a[...], wa_ref[...], preferred_element_type=F32)
                              + jnp.dot(b[...], wb_ref[...], preferred_element_type=F32))
            if cond is None:
                add_mix()
            else:
                pl.when(cond)(add_mix)
        x_ref = o_ref
    else:
        x_ref, g_ref, wup_ref, wdn_ref, o_ref, h_sc, acc_sc = refs
    h_sc[...] = _rms(x_ref[...], g_ref[...]).astype(BF16)
    d_ff = wdn_ref.shape[1]
    for c in range(d_ff // FF_CHUNK):
        lo, hi = c * FF_CHUNK, (c + 1) * FF_CHUNK
        g = jnp.dot(h_sc[...], wup_ref[0, :, lo:hi], preferred_element_type=F32)
        u = jnp.dot(h_sc[...], wup_ref[0, :, d_ff + lo:d_ff + hi], preferred_element_type=F32)
        act = (g * _sigmoid(g) * u).astype(BF16)
        down = jnp.dot(act, wdn_ref[0, lo:hi, :], preferred_element_type=F32)
        if c == 0:
            acc_sc[...] = down
        else:
            acc_sc[...] += down
    o_ref[...] = x_ref[...] + 0.5 * acc_sc[...]


def _ffn(x, g, w_up, w_down, layer, row0=0, nrows=None, mix=None, in_place=False):
    t, d = x.shape
    nrows = t if nrows is None else nrows
    d_ff = w_down.shape[1]
    tm = FFN_ROW_TILE
    rb0 = row0 // tm
    resident = pl.Buffered(1)
    in_specs = [pl.BlockSpec((tm, d), lambda i: (rb0 + i, 0))]
    args = [x]
    first_tiles = None
    if mix is not None:
        a, wa, b, wb = mix[:4]
        ka, kb = a.shape[1], b.shape[1]
        last_a = a.shape[0] // tm - 1
        in_specs += [
            pl.BlockSpec((tm, ka), lambda i: (jnp.minimum(i, last_a), 0)),
            pl.BlockSpec((ka, d), lambda i: (0, 0), pipeline_mode=resident),
            pl.BlockSpec((tm, kb), lambda i: (jnp.minimum(i, last_a), 0)),
            pl.BlockSpec((kb, d), lambda i: (0, 0), pipeline_mode=resident),
        ]
        args += [a, wa, b, wb]
        if len(mix) > 4:
            first_tiles = last_a + 1
            in_specs += [pl.BlockSpec((tm, ka), lambda i: (jnp.maximum(i - first_tiles, 0), 0)),
                         pl.BlockSpec((tm, kb), lambda i: (jnp.maximum(i - first_tiles, 0), 0))]
            args += list(mix[4:])
    in_specs += [
        pl.BlockSpec((1, d), lambda i: (0, 0)),
        pl.BlockSpec((1, d, 2 * d_ff), lambda i: (layer, 0, 0), pipeline_mode=resident),
        pl.BlockSpec((1, d_ff, d), lambda i: (layer, 0, 0), pipeline_mode=resident),
    ]
    args += [g.reshape(1, d), w_up, w_down]
    out_rb0, out_rows = (rb0, t) if in_place else (0, nrows)
    return pl.pallas_call(
        functools.partial(_ffn_kernel, mixed=mix is not None, first_tiles=first_tiles),
        grid=(nrows // tm,),
        in_specs=in_specs,
        out_specs=pl.BlockSpec((tm, d), lambda i: (out_rb0 + i, 0)),
        out_shape=jax.ShapeDtypeStruct((out_rows, d), F32),
        scratch_shapes=[pltpu.VMEM((tm, d), BF16), pltpu.VMEM((tm, d), F32)],
        input_output_aliases={0: 0} if in_place else {},
        compiler_params=_cparams(("parallel",)),
        name="ffn",
    )(*args)


def _proj_kernel(x_ref, g_ref, w_ref, gain_ref, o_ref, h_sc, *, normed_groups, head_width):
    h_sc[...] = _rms(x_ref[...], g_ref[...]).astype(BF16)
    for gi in range(w_ref.shape[2] // GROUP):
        sl = slice(gi * GROUP, (gi + 1) * GROUP)
        o_ref[:, sl] = jnp.dot(h_sc[...], w_ref[0, :, sl], preferred_element_type=F32)
    squares = [(o_ref[:, gi * GROUP:(gi + 1) * GROUP] ** 2).astype(BF16) for gi in normed_groups]
    sums = [jnp.dot(sq, _head_group_matrix(GROUP, head_width), preferred_element_type=F32) for sq in squares]
    for gi, ss in zip(normed_groups, sums):
        sl = slice(gi * GROUP, (gi + 1) * GROUP)
        o_ref[:, sl] = o_ref[:, sl] * lax.rsqrt(ss * (1.0 / head_width) + EPS) * gain_ref[:, sl]


def _proj(x, g, w, layer, gain, normed_groups, tm=ROW_TILE, head_width=HD_B):
    t, d = x.shape
    n = w.shape[2]
    return pl.pallas_call(
        functools.partial(_proj_kernel, normed_groups=tuple(normed_groups), head_width=head_width),
        grid=(t // tm,),
        in_specs=[
            pl.BlockSpec((tm, d), lambda i: (i, 0)),
            pl.BlockSpec((1, d), lambda i: (0, 0)),
            pl.BlockSpec((1, d, n), lambda i: (layer, 0, 0), pipeline_mode=pl.Buffered(1)),
            pl.BlockSpec((1, n), lambda i: (0, 0)),
        ],
        out_specs=pl.BlockSpec((tm, n), lambda i: (i, 0)),
        out_shape=jax.ShapeDtypeStruct((t, n), F32),
        scratch_shapes=[pltpu.VMEM((tm, d), BF16)],
        compiler_params=_cparams(("parallel",)),
        name="proj",
    )(x, g.reshape(1, d), w, gain.reshape(1, n))


def _memattn_kernel(q_ref, k_ref, v_ref, o_ref):
    q = q_ref[...]
    kb = k_ref[...].astype(BF16)
    vb = v_ref[...].astype(BF16)
    lane = lax.broadcasted_iota(jnp.int32, q.shape, 1) // HD_MEM
    scores = []
    for h in range(H_MEM):
        qh = jnp.where(lane == h, q, 0.0).astype(BF16)
        scores.append(lax.dot_general(qh, kb, (((1,), (1,)), ((), ())), preferred_element_type=F32)
                      * (HD_MEM ** -0.5))
    probs = []
    for s in scores:
        e = jnp.exp(s - jnp.max(s, axis=-1, keepdims=True))
        probs.append((e / jnp.sum(e, axis=-1, keepdims=True)).astype(BF16))
    acc = jnp.zeros(q.shape, F32)
    for h, p in enumerate(probs):
        acc = acc + jnp.where(lane == h, jnp.dot(p, vb, preferred_element_type=F32), 0.0)
    o_ref[...] = acc.astype(o_ref.dtype)


def _memattn(qarr, qcol, row0, nb, lq, tq, karr, kcol, varr, vcol, out_dtype):
    m = karr.shape[0] // nb
    nq = lq // tq
    rb0 = row0 // tq
    return pl.pallas_call(
        _memattn_kernel,
        grid=(nb, nq),
        in_specs=[
            pl.BlockSpec((tq, D_MEM), lambda b, i: (rb0 + b * nq + i, qcol)),
            pl.BlockSpec((m, D_MEM), lambda b, i: (b, kcol)),
            pl.BlockSpec((m, D_MEM), lambda b, i: (b, vcol)),
        ],
        out_specs=pl.BlockSpec((tq, D_MEM), lambda b, i: (b * nq + i, 0)),
        out_shape=jax.ShapeDtypeStruct((nb * lq, D_MEM), out_dtype),
        compiler_params=_cparams(("parallel", "parallel")),
        name="memattn",
    )(qarr, karr, varr)


def _memattn_small_kernel(q_ref, k_ref, v_ref, o_ref, *, nseq, lq):
    lane = lax.broadcasted_iota(jnp.int32, (lq, D_MEM), 1) // HD_MEM
    scores = []
    for s in range(nseq):
        q = q_ref[s * lq:(s + 1) * lq, :]
        qx = jnp.concatenate([jnp.where(lane == h, q, 0.0) for h in range(H_MEM)], axis=0).astype(BF16)
        kt = k_ref[s * D_MEM:(s + 1) * D_MEM, :].astype(BF16)
        scores.append(jnp.dot(qx, kt, preferred_element_type=F32) * (HD_MEM ** -0.5))
    probs = []
    for sc in scores:
        e = jnp.exp(sc - jnp.max(sc, axis=-1, keepdims=True))
        probs.append((e / jnp.sum(e, axis=-1, keepdims=True)).astype(BF16))
    for s, p in enumerate(probs):
        vt = v_ref[s * D_MEM:(s + 1) * D_MEM, :].astype(BF16)
        ox = lax.dot_general(p, vt, (((1,), (1,)), ((), ())), preferred_element_type=F32)
        out = jnp.zeros((lq, D_MEM), F32)
        for h in range(H_MEM):
            out = out + jnp.where(lane == h, ox[h * lq:(h + 1) * lq, :], 0.0)
        o_ref[s * lq:(s + 1) * lq, :] = out


def _memattn_small(qarr, qcol, row0, nb, lq, m, ktarr, vtarr, seq0, nseq=8):
    rb0 = row0 // (nseq * lq)
    kb0 = seq0 // nseq
    return pl.pallas_call(
        functools.partial(_memattn_small_kernel, nseq=nseq, lq=lq),
        grid=(nb // nseq,),
        in_specs=[
            pl.BlockSpec((nseq * lq, D_MEM), lambda i: (rb0 + i, qcol)),
            pl.BlockSpec((nseq * D_MEM, m), lambda i: (kb0 + i, 0)),
            pl.BlockSpec((nseq * D_MEM, m), lambda i: (kb0 + i, 0)),
        ],
        out_specs=pl.BlockSpec((nseq * lq, D_MEM), lambda i: (i, 0)),
        out_shape=jax.ShapeDtypeStruct((nb * lq, D_MEM), F32),
        compiler_params=_cparams(("parallel",)),
        name="memattn_small",
    )(qarr, ktarr, vtarr)


def _mlstm_kernel(qkvo_ref, gates_ref, convprev_ref, c0_ref, n0_ref, m0_ref,
                  convw_ref, convb_ref, bgate_ref, gh_ref, c_all_ref,
                  h_ref, c_out_ref, n_out_ref, m_out_ref,
                  ext_sc, c_sc, n_sc, m_sc, *, lb, lp, slot):
    ci = pl.program_id(1)
    last = lb - 1

    @pl.when(ci == 0)
    def _():
        ext_sc[0:8, :] = convprev_ref[0]
        c_sc[...] = jnp.zeros_like(c_sc)
        n_sc[...] = jnp.zeros_like(n_sc)
        for h in range(H_A):
            c_sc[h, 0:HD_A, 0:HD_A] = c0_ref[0, 0, h]
        n_sc[0:H_A, 0:HD_A] = n0_ref[0]
        m_sc[0:H_A, :] = m0_ref[0]

    def pad_rows(a):
        return a if lb == lp else jnp.concatenate([a, jnp.zeros((lp - lb, a.shape[1]), a.dtype)], axis=0)

    ext_sc[8:8 + lb, :] = qkvo_ref[:, 0:2 * D_A_PAD]

    row = lax.broadcasted_iota(jnp.int32, (lp, lp), 0)
    col = lax.broadcasted_iota(jnp.int32, (lp, lp), 1)
    causal = row >= col
    valid_row = lax.broadcasted_iota(jnp.int32, (lp, 1), 0) < lb

    gl = lax.broadcasted_iota(jnp.int32, (lp, GROUP), 1)
    gts = pad_rows(gates_ref[...]) + bgate_ref[...]
    gts = jnp.where((gl >= H_A) & (gl < 2 * H_A), -_softplus(-gts), gts)
    gts = jnp.where(valid_row & (gl < 2 * H_A), gts, 0.0)
    cum = jnp.dot(causal.astype(F32), gts, preferred_element_type=F32, precision=lax.Precision.HIGHEST)
    gts_t = gts.T
    cum_t = cum.T

    def conv_silu(c0):
        sl = slice(c0, c0 + HD_A_PAD)
        a = convb_ref[:, sl]
        for j in range(CONV_W):
            a = a + ext_sc[pl.ds(8 - (CONV_W - 1) + j, lb), sl] * convw_ref[j:j + 1, sl]
        return pad_rows(a * _sigmoid(a))

    heads = []
    for h in range(H_A):
        q = conv_silu(h * HD_A_PAD)
        k = conv_silu(D_A_PAD + h * HD_A_PAD) * (HD_A ** -0.5)
        vb = pad_rows(qkvo_ref[:, 2 * D_A_PAD + h * HD_A_PAD:2 * D_A_PAD + (h + 1) * HD_A_PAD]).astype(BF16)
        qb, kb = q.astype(BF16), k.astype(BF16)
        s = lax.dot_general(qb, kb, (((1,), (1,)), ((), ())), preferred_element_type=F32)
        cq = lax.dot_general(qb, c_sc[h].astype(BF16), (((1,), (1,)), ((), ())), preferred_element_type=F32)
        qn = jnp.sum(q * n_sc[h:h + 1, :], axis=-1, keepdims=True)
        heads.append((k, vb, s, cq, qn))

    for h, (k, vb, s, cq, qn) in enumerate(heads):
        hs = slice(h * HD_A_PAD, (h + 1) * HD_A_PAD)
        o_pre = qkvo_ref[:, 3 * D_A_PAD + h * HD_A_PAD:3 * D_A_PAD + (h + 1) * HD_A_PAD]

        ig_col = gts[:, h:h + 1]
        b_col = cum[:, H_A + h:H_A + h + 1]
        ig_row = gts_t[h:h + 1, :]
        b_row = cum_t[H_A + h:H_A + h + 1, :]
        m_prev = m_sc[h:h + 1, 0:1]
        c_prev = c_sc[h]
        n_prev = n_sc[h:h + 1, :]

        dlog = jnp.where(causal, b_col - b_row + ig_row, -jnp.inf)
        inter = b_col + m_prev
        m_t = jnp.maximum(inter, jnp.max(dlog, axis=-1, keepdims=True))
        w = jnp.exp(dlog - m_t) * s
        w_inter = jnp.exp(inter - m_t)
        num = jnp.dot(w.astype(BF16), vb, preferred_element_type=F32) + cq * w_inter
        den = jnp.sum(w, axis=-1, keepdims=True) + w_inter * qn
        den = jnp.maximum(jnp.abs(den), jnp.exp(-m_t))
        hh = num / den
        hn = hh * lax.rsqrt(jnp.sum(hh * hh, axis=-1, keepdims=True) * (1.0 / HD_A) + EPS) * gh_ref[:, hs]
        h_ref[:, hs] = (hn[0:lb, :] * _sigmoid(o_pre)).astype(h_ref.dtype)

        m_new = m_t[last:last + 1, :]
        b_last = b_col[last:last + 1, :]
        w_end = jnp.exp(b_last - b_col + ig_col - m_new)
        decay = jnp.exp(b_last + m_prev - m_new)
        kw = k * w_end
        c_sc[h] = decay * c_prev + lax.dot_general(vb, kw.astype(BF16), (((0,), (0,)), ((), ())),
                                                   preferred_element_type=F32)
        n_sc[h:h + 1, :] = decay * n_prev + jnp.sum(kw, axis=0, keepdims=True)
        m_sc[h:h + 1, :] = jnp.broadcast_to(m_new, (1, LANES))

    if lb == lp:
        ext_sc[0:8, :] = ext_sc[lp:lp + 8, :]

    @pl.when(ci == pl.num_programs(1) - 1)
    def _():
        for s in range(c_out_ref.shape[0]):
            if s != slot:
                c_out_ref[s] = jnp.zeros(c_out_ref.shape[1:], F32)
        for h in range(H_A):
            c_out_ref[slot, 0, h] = c_sc[h, 0:HD_A, 0:HD_A]
        n_out_ref[0] = n_sc[0:H_A, 0:HD_A]
        m_out_ref[0] = m_sc[0:H_A, :]


def _mlstm(proj, row0, nb, seq, lb, lp, convprev, c0, c0_slot, c_out_prev, layer, n_layers, n0, m0,
           convw, convb, bgate, gh, out_dtype):
    creates = c_out_prev is None
    nc = seq // lb
    rb0 = row0 // lb
    gate_col = (4 * D_A_PAD + D_MEM) // GROUP
    scratch = [
        pltpu.VMEM((lp + 8, 2 * D_A_PAD), F32),
        pltpu.VMEM((H_A, HD_A_PAD, HD_A_PAD), F32),
        pltpu.VMEM((8, HD_A_PAD), F32),
        pltpu.VMEM((8, LANES), F32),
    ]
    assert lb == lp or nc == 1, "a zero-padded tile must be the only chunk of its sequence"
    in_specs = [
        pl.BlockSpec((lb, 4 * D_A_PAD), lambda b, c: (rb0 + b * nc + c, 0)),
        pl.BlockSpec((lb, GROUP), lambda b, c: (rb0 + b * nc + c, gate_col)),
        pl.BlockSpec((1, 8, 2 * D_A_PAD), lambda b, c: (b, 0, 0)),
        pl.BlockSpec((1, 1, H_A, HD_A, HD_A), lambda b, c: (c0_slot, b, 0, 0, 0)),
        pl.BlockSpec((1, H_A, HD_A), lambda b, c: (b, 0, 0)),
        pl.BlockSpec((1, H_A, LANES), lambda b, c: (b, 0, 0)),
        pl.BlockSpec((CONV_W, 2 * D_A_PAD), lambda b, c: (0, 0)),
        pl.BlockSpec((1, 2 * D_A_PAD), lambda b, c: (0, 0)),
        pl.BlockSpec((1, GROUP), lambda b, c: (0, 0)),
        pl.BlockSpec((1, D_A_PAD), lambda b, c: (0, 0)),
    ]
    in_specs.append(pl.BlockSpec(memory_space=pl.ANY))
    args = [proj, proj, convprev, c0, n0, m0, convw, convb, bgate, gh, c0 if creates else c_out_prev]
    aliases = {} if creates else {len(args) - 1: 1}
    c_block = (n_layers, 0) if creates else (1, layer)
    return pl.pallas_call(
        functools.partial(_mlstm_kernel, lb=lb, lp=lp, slot=layer if creates else 0),
        grid=(nb, nc),
        in_specs=in_specs,
        out_specs=[
            pl.BlockSpec((lb, D_A_PAD), lambda b, c: (b * nc + c, 0)),
            pl.BlockSpec((c_block[0], 1, H_A, HD_A, HD_A), lambda b, c: (c_block[1], b, 0, 0, 0)),
            pl.BlockSpec((1, H_A, HD_A), lambda b, c: (b, 0, 0)),
            pl.BlockSpec((1, H_A, LANES), lambda b, c: (b, 0, 0)),
        ],
        out_shape=[
            jax.ShapeDtypeStruct((nb * seq, D_A_PAD), out_dtype),
            jax.ShapeDtypeStruct((n_layers, nb, H_A, HD_A, HD_A), F32),
            jax.ShapeDtypeStruct((nb, H_A, HD_A), F32),
            jax.ShapeDtypeStruct((nb, H_A, LANES), F32),
        ],
        scratch_shapes=scratch,
        input_output_aliases=aliases,
        compiler_params=_cparams(("parallel", "arbitrary")),
        name="mlstm",
    )(*args)


SB_BLOCK = 256
SB_QBLOCK = 1024
LOG2E = 1.4426950408889634


def _neg_later(n, keys_on_rows):
    r = lax.broadcasted_iota(jnp.int32, (n, n), 0)
    c = lax.broadcasted_iota(jnp.int32, (n, n), 1)
    later = (c > r) if keys_on_rows else (r > c)
    return jnp.where(later, -1.0, 0.0).astype(BF16)


def _softplus2(z):
    return jnp.where(z > 64.0, z, jnp.log(1.0 + jnp.exp2(z)) * LOG2E)


def _sb_weights(z_streams, masks, later_mat, carries, keys_on_rows=False, row0s=None):
    first = (lambda x: x[0:1, :]) if keys_on_rows else (lambda x: x[:, 0:1])
    row0s = [0] * len(masks) if row0s is None else row0s

    own, sp_first, inblock = [], [], []
    for zs in z_streams:
        o_s, f_s, i_s = [], [], []
        for z, m in zip(zs, masks):
            sp = _softplus2(z) if m is None else jnp.where(m, _softplus2(z), 0.0)
            o_s.append(z - sp)
            f_s.append(first(sp))
            spb = sp.astype(BF16)
            i_s.append(jnp.dot(later_mat, spb, preferred_element_type=F32) if keys_on_rows
                       else jnp.dot(spb, later_mat, preferred_element_type=F32))
        own.append(o_s)
        sp_first.append(f_s)
        inblock.append(i_s)
    weights, new_carries = [], []
    for o_s, f_s, i_s, carry in zip(own, sp_first, inblock, carries):
        ws = []
        for o, f, l, m, r0 in zip(o_s, f_s, i_s, masks, row0s):
            later = l + carry[r0:]
            a = jnp.exp2(o + later)
            ws.append(a if m is None else jnp.where(m, a, 0.0))
            new = first(later) - f
            carry = new if r0 == 0 else jnp.concatenate([carry[:r0], new], axis=0)
        weights.append(ws)
        new_carries.append(carry)
    return weights, tuple(new_carries)


def _sb_prompt_kernel(bias_ref, q_ref, k_ref, v_ref, o_ref, *rest, emit_kv_t):
    p = pl.program_id(1)
    i = pl.program_id(2)
    blk = SB_BLOCK
    nh = LANES // HD_B
    if emit_kv_t:
        kt_ref, vt_ref = rest[:2]
        rest = rest[2:]
    k_scs, v_scs = rest[:nh], rest[nh:]
    spare = [((hh + 1) % nh) * HD_B for hh in range(nh)]

    @pl.when(i == 0)
    def _():
        k = k_ref[...]
        v = v_ref[...]
        lane_id = lax.broadcasted_iota(jnp.int32, k.shape, 1)
        for hh in range(nh):
            ones = (lane_id == spare[hh]) | (lane_id == spare[hh] + 1)
            k_scs[hh][...] = jnp.where(lane_id // HD_B == hh, k, jnp.where(ones, 1.0, 0.0)).astype(BF16)
            v_scs[hh][...] = jnp.where(lane_id // HD_B == hh, v, 0.0).astype(BF16)
        if emit_kv_t:
            kt_ref[0] = k.T
            vt_ref[0] = v.T

    q = q_ref[...]
    tq = q.shape[0]
    per_q = tq // blk
    lane_id = lax.broadcasted_iota(jnp.int32, q.shape, 1)
    later_mat = _neg_later(blk, keys_on_rows=False)
    qh = []
    for hh in range(nh):
        b2 = jnp.full((1, LANES), bias_ref[p * nh + hh] * LOG2E, F32)
        b2_hi = b2.astype(BF16).astype(F32)
        extra = jnp.where(lane_id == spare[hh], b2_hi, jnp.where(lane_id == spare[hh] + 1, b2 - b2_hi, 0.0))
        qh.append(jnp.where(lane_id // HD_B == hh, q * (HD_B ** -0.5 * LOG2E), extra).astype(BF16))

    def tiles(js, masks, row0s, carries, acc):
        offs = [pl.multiple_of(j * blk, blk) for j in js]
        zs = [[lax.dot_general(qh[hh][r0:], k_scs[hh][pl.ds(off, blk), :], (((1,), (1,)), ((), ())),
                               preferred_element_type=F32) for off, r0 in zip(offs, row0s)] for hh in range(nh)]
        ws, carries = _sb_weights(zs, masks, later_mat, carries, row0s=row0s)
        for hh in range(nh):
            for off, a, r0 in zip(offs, ws[hh], row0s):
                upd = jnp.dot(a.astype(BF16), v_scs[hh][pl.ds(off, blk), :], preferred_element_type=F32)
                acc = acc + (upd if r0 == 0 else jnp.concatenate([jnp.zeros((r0, LANES), F32), upd], axis=0))
        return carries, acc

    rel = lax.broadcasted_iota(jnp.int32, (tq, blk), 0) - lax.broadcasted_iota(jnp.int32, (tq, blk), 1)
    first = i * per_q
    order = list(reversed(range(per_q)))
    zero_c = tuple(jnp.zeros((tq, 1), F32) for _ in range(nh))
    carries, acc = tiles([first + t for t in order], [rel[t * blk:] > t * blk for t in order],
                         [t * blk for t in order], zero_c, jnp.zeros(q.shape, F32))

    def body(t, state):
        j = first - 1 - t * per_q
        return tiles([j - u for u in range(per_q)], [None] * per_q, [0] * per_q, state[0], state[1])

    _, acc = lax.fori_loop(0, i, body, (carries, acc))
    o_ref[...] = acc.astype(o_ref.dtype)


def _sb_prompt(qarr, kvarr, bias, nb, seq, emit_kv_t):
    blk = SB_QBLOCK
    npair = D_MAIN // LANES
    ni = seq // blk
    out_specs = [pl.BlockSpec((blk, LANES), lambda b, p, i, s: (b * ni + i, p))]
    out_shape = [jax.ShapeDtypeStruct((nb * seq, D_MAIN), BF16)]
    if emit_kv_t:
        out_specs += [pl.BlockSpec((1, LANES, seq), lambda b, p, i, s: (b, p, 0))] * 2
        out_shape += [jax.ShapeDtypeStruct((nb, D_MAIN, seq), F32)] * 2
    grid_spec = pltpu.PrefetchScalarGridSpec(
        num_scalar_prefetch=1,
        grid=(nb, npair, ni),
        in_specs=[
            pl.BlockSpec((blk, LANES), lambda b, p, i, s: (b * ni + i, p)),
            pl.BlockSpec((seq, LANES), lambda b, p, i, s: (b, p)),
            pl.BlockSpec((seq, LANES), lambda b, p, i, s: (b, npair + p)),
        ],
        out_specs=out_specs,
        scratch_shapes=[pltpu.VMEM((seq, LANES), BF16)] * (2 * LANES // HD_B),
    )
    return pl.pallas_call(
        functools.partial(_sb_prompt_kernel, emit_kv_t=emit_kv_t),
        grid_spec=grid_spec,
        out_shape=out_shape,
        compiler_params=_cparams(("parallel", "parallel", "arbitrary")),
        name="sb_prompt",
    )(bias, qarr, kvarr, kvarr)


def _sb_sample_kernel(pt_ref, q_ref, knew_ref, vnew_ref, *refs, lq, page, npage):
    kt_refs, vt_refs = refs[:npage], refs[npage:2 * npage]
    bias_ref, o_ref, pad_sc = refs[2 * npage:]
    rows = H_B * lq
    later_mat = _neg_later(page, keys_on_rows=True)
    tn_dims = (((0,), (0,)), ((), ()))

    q = q_ref[...]
    lane = lax.broadcasted_iota(jnp.int32, q.shape, 1) // HD_B
    qx = jnp.concatenate([jnp.where(lane == h, q, 0.0) for h in range(H_B)]
                         + [jnp.zeros((page - rows, D_MAIN), F32)], axis=0)
    qxt = (qx * (HD_B ** -0.5 * LOG2E)).T.astype(BF16)
    bias = bias_ref[...] * LOG2E

    pad_sc[...] = jnp.zeros_like(pad_sc)
    pad_sc[0:lq, :] = knew_ref[...]
    knew = pad_sc[...].astype(BF16)
    pad_sc[0:lq, :] = vnew_ref[...]
    vnew = pad_sc[...].astype(BF16)
    kidx = lax.broadcasted_iota(jnp.int32, (page, page), 0)
    qidx = lax.broadcasted_iota(jnp.int32, (page, page), 1) % lq
    order = list(reversed(range(npage)))

    zs = [jnp.dot(knew, qxt, preferred_element_type=F32) + bias]
    zs += [lax.dot_general(kt_refs[g][0].astype(BF16), qxt, tn_dims, preferred_element_type=F32) + bias
           for g in order]
    masks = [kidx < qidx] + [None] * npage
    (ws,), _ = _sb_weights([zs], masks, later_mat, (jnp.zeros((1, page), F32),), keys_on_rows=True)

    acc = lax.dot_general(vnew, ws[0].astype(BF16), tn_dims, preferred_element_type=F32)
    for g, a in zip(order, ws[1:]):
        acc = acc + jnp.dot(vt_refs[g][0].astype(BF16), a.astype(BF16), preferred_element_type=F32)

    acc_t = acc.T
    olane = lax.broadcasted_iota(jnp.int32, (lq, D_MAIN), 1) // HD_B
    out = jnp.zeros((lq, D_MAIN), F32)
    for h in range(H_B):
        out = out + jnp.where(olane == h, acc_t[h * lq:(h + 1) * lq, :], 0.0)
    o_ref[...] = out


def _sb_sample(qarr, kvarr, row0, nb, lq, cache_kt, cache_vt, page_table, bias_cols):
    npage = page_table.shape[1]
    page = cache_kt.shape[2]
    rb0 = row0 // lq
    page_specs = [pl.BlockSpec((1, D_MAIN, page), lambda b, pt, g=g: (pt[b, g], 0, 0)) for g in range(npage)]
    grid_spec = pltpu.PrefetchScalarGridSpec(
        num_scalar_prefetch=1,
        grid=(nb,),
        in_specs=[
            pl.BlockSpec((lq, D_MAIN), lambda b, pt: (rb0 + b, 0)),
            pl.BlockSpec((lq, D_MAIN), lambda b, pt: (rb0 + b, 0)),
            pl.BlockSpec((lq, D_MAIN), lambda b, pt: (rb0 + b, 1)),
        ] + page_specs + page_specs + [pl.BlockSpec((1, page), lambda b, pt: (0, 0))],
        out_specs=pl.BlockSpec((lq, D_MAIN), lambda b, pt: (b, 0)),
        scratch_shapes=[pltpu.VMEM((page, D_MAIN), F32)],
    )
    return pl.pallas_call(
        functools.partial(_sb_sample_kernel, lq=lq, page=page, npage=npage),
        grid_spec=grid_spec,
        out_shape=jax.ShapeDtypeStruct((nb * lq, D_MAIN), F32),
        compiler_params=_cparams(("parallel",)),
        name="sb_sample",
    )(page_table, qarr, kvarr, kvarr, *([cache_kt] * npage), *([cache_vt] * npage), bias_cols)


def _pad_heads(w):
    lead, nh = w.shape[:-1], w.shape[-1] // HD_A
    w = jnp.pad(w.reshape(lead + (nh, HD_A)), [(0, 0)] * (len(lead) + 1) + [(0, HD_A_PAD - HD_A)])
    return w.reshape(lead + (nh * HD_A_PAD,))


def _unpad_heads(w, nh):
    lead = w.shape[:-1]
    return w.reshape(lead + (nh, HD_A_PAD))[..., :HD_A].reshape(lead + (nh * HD_A,))


def kernel(x_prompt, x_sample, cache_k, cache_v, cache_mem_k, cache_mem_v, state_C, state_n, state_m, state_conv, page_table, mem_prompt, g_ffn1, w_ffn1_up, w_ffn1_down, g_mix, w_in_a, b_gate_a, conv_w_a, conv_b_a, g_h_a, g_kv, w_kv, g_k_b, w_in_b, g_q_b, b_sb, g_mem, w_mem_kv, g_q_mem, g_k_mem, w_out, g_ffn2, w_ffn2_up, w_ffn2_down):
    bp, seq, d = x_prompt.shape
    bs, lq, _ = x_sample.shape
    depth = g_ffn1.shape[0]
    n_a = w_in_a.shape[0]
    tp, ts = bp * seq, bs * lq
    mem_len = mem_prompt.shape[1]
    n_phys, page = cache_k.shape[0], cache_k.shape[1]

    x = jnp.concatenate([x_prompt.reshape(tp, d), x_sample.reshape(bs * lq, d)], axis=0)
    wu1, wd1 = w_ffn1_up.astype(BF16), w_ffn1_down.astype(BF16)
    wu2, wd2 = w_ffn2_up.astype(BF16), w_ffn2_down.astype(BF16)
    wo_b, w_kv_b, w_in_b_b, w_mem_b = (w.astype(BF16) for w in (w_out, w_kv[None], w_in_b, w_mem_kv))
    split = 4 * D_MAIN
    w_a_all = jnp.concatenate([
        _pad_heads(w_in_a[:, :, :split]), w_in_a[:, :, split + 2 * H_A:],
        jnp.pad(w_in_a[:, :, split:split + 2 * H_A], ((0, 0), (0, 0), (0, GROUP - 2 * H_A)))], axis=2).astype(BF16)
    w_main_a = jnp.pad(wo_b[:n_a, :D_MAIN].reshape(n_a, H_A, HD_A, d),
                       ((0, 0), (0, 0), (0, HD_A_PAD - HD_A), (0, 0))).reshape(n_a, D_A_PAD, d)
    convw_all, convb_all = _pad_heads(conv_w_a), _pad_heads(conv_b_a)[:, None, :]
    bgate_all = jnp.pad(b_gate_a, ((0, 0), (0, GROUP - 2 * H_A)))[:, None, :]
    gh_all = _pad_heads(g_h_a.reshape(n_a, H_A * HD_A))[:, None, :]

    mem_rows = mem_prompt.reshape(bp * mem_len, d)
    mem_kv_p = []
    for l in range(depth):
        gain = jnp.concatenate([jnp.tile(g_k_mem[l], H_MEM), jnp.ones((D_MEM,), F32)])
        mem_kv_p.append(_proj(mem_rows, g_mem[l], w_mem_b, l, gain, (0,), tm=bp * mem_len, head_width=HD_MEM))
    cmkt = cache_mem_k.transpose(0, 1, 3, 4, 2).reshape(depth * bs * D_MEM, mem_len)
    cmvt = cache_mem_v.transpose(0, 1, 3, 4, 2).reshape(depth * bs * D_MEM, mem_len)
    ckt = cache_k.transpose(0, 2, 3, 1).reshape(n_phys, D_MAIN, page)
    cvt = cache_v.transpose(0, 2, 3, 1).reshape(n_phys, D_MAIN, page)

    convs_p, convs_s, ns_p, ns_s, ms_p, ms_s = [], [], [], [], [], []
    c_p = c_s = None
    c0_p = jnp.zeros((1, bp, H_A, HD_A, HD_A), F32)
    kv = None
    for l in range(depth):
        if l == n_a:
            gain = jnp.concatenate([jnp.tile(g_k_b, H_B), jnp.ones((D_MAIN,), F32)])
            kv = _proj(x, g_kv, w_kv_b, 0, gain, tuple(range(D_MAIN // GROUP)))
        x = _ffn(x, g_ffn1[l], wu1, wd1, l)
        if l < n_a:
            gain = jnp.concatenate([jnp.ones((4 * D_A_PAD,), F32), jnp.tile(g_q_mem[l], H_MEM),
                                    jnp.ones((GROUP,), F32)])
            proj = _proj(x, g_mix[l], w_a_all, l, gain, (4 * D_A_PAD // GROUP,), tm=ROW_TILE // 2,
                         head_width=HD_MEM)
            qm_col = 4 * D_A_PAD // D_MEM
            convw, convb, bgate, gh = convw_all[l], convb_all[l], bgate_all[l], gh_all[l]

            def conv_state(c):
                return jnp.pad(_pad_heads(c), ((0, 0), (8 - (CONV_W - 1), 0), (0, 0)))

            main_p, c_p, n_p, m_p = _mlstm(
                proj, 0, bp, seq, SB_BLOCK, SB_BLOCK,
                conv_state(jnp.zeros((bp, CONV_W - 1, 2 * D_MAIN), F32)), c0_p, 0, c_p, l, n_a,
                jnp.zeros((bp, H_A, HD_A), F32), jnp.zeros((bp, H_A, LANES), F32),
                convw, convb, bgate, gh, BF16)
            main_s, c_s, n_s, m_s = _mlstm(
                proj, tp, bs, lq, lq, LANES, conv_state(state_conv[l]), state_C, l, c_s, l, n_a, state_n[l],
                jnp.broadcast_to(state_m[l][..., None], (bs, H_A, LANES)), convw, convb, bgate, gh, F32)
            w_main = w_main_a[l]

            tail = CONV_W - 1
            conv_rows_p = jnp.stack([proj[(b + 1) * seq - tail:(b + 1) * seq, :2 * D_A_PAD] for b in range(bp)])
            conv_rows_s = proj[tp:].reshape(bs, lq, -1)[:, lq - tail:, :2 * D_A_PAD]
            convs_p.append(_unpad_heads(conv_rows_p, 2 * H_A))
            convs_s.append(_unpad_heads(conv_rows_s, 2 * H_A))
            ns_p.append(n_p); ms_p.append(m_p[..., 0])
            ns_s.append(n_s); ms_s.append(m_s[..., 0])
        else:
            j = l - n_a
            gain = jnp.concatenate([jnp.tile(g_q_b[j], H_B), jnp.tile(g_q_mem[l], H_MEM)])
            proj = _proj(x, g_mix[l], w_in_b_b, j, gain, tuple(range(D_MODEL // GROUP)))
            qm_col = D_MAIN // D_MEM
            sb_out = _sb_prompt(proj, kv, b_sb[j], bp, seq, emit_kv_t=(j == 0))
            main_p = sb_out[0]
            if j == 0:
                k_prompt, v_prompt = [a.reshape(bp, H_B, HD_B, seq).transpose(0, 3, 1, 2) for a in sb_out[1:]]
            bias_cols = jnp.pad(jnp.repeat(b_sb[j], lq), (0, page - H_B * lq)).reshape(1, page)
            main_s = _sb_sample(proj, kv, tp, bs, lq, ckt, cvt, page_table, bias_cols)
            w_main = wo_b[l, :D_MAIN]

        mem_p = _memattn(proj, qm_col, 0, bp, seq, 512, mem_kv_p[l], 0, mem_kv_p[l], 1, BF16)
        mem_s = _memattn_small(proj, qm_col, tp, bs, lq, mem_len, cmkt, cmvt, l * bs)
        w_mem = wo_b[l, D_MAIN:]
        mix_p = (main_p, w_main, mem_p, w_mem)
        mix_s = (main_s.astype(BF16), w_main, mem_s.astype(BF16), w_mem)
        if l < depth - 1:
            x = _ffn(x, g_ffn2[l], wu2, wd2, l, mix=mix_p + (mix_s[0], mix_s[2]), in_place=True)
        else:
            y_p = _ffn(x, g_ffn2[l], wu2, wd2, l, row0=0, nrows=tp, mix=mix_p)
            y_s = _ffn(x, g_ffn2[l], wu2, wd2, l, row0=tp, nrows=ts, mix=mix_s)

    mem_k_prompt = jnp.stack([m[:, :D_MEM] for m in mem_kv_p]).reshape(depth, bp, mem_len, H_MEM, HD_MEM)
    mem_v_prompt = jnp.stack([m[:, D_MEM:] for m in mem_kv_p]).reshape(depth, bp, mem_len, H_MEM, HD_MEM)
    return (y_p.reshape(bp, seq, d), y_s.reshape(bs, lq, d),
            k_prompt, v_prompt,
            kv[tp:, :D_MAIN].reshape(bs, lq, H_B, HD_B), kv[tp:, D_MAIN:].reshape(bs, lq, H_B, HD_B),
            mem_k_prompt, mem_v_prompt,
            jnp.stack(convs_p), c_p, jnp.stack(ns_p), jnp.stack(ms_p),
            jnp.stack(convs_s), c_s, jnp.stack(ns_s), jnp.stack(ms_s))
```
